```python
import math
import jax, jax.numpy as jnp
from jax import lax
import numpy as np

D_MODEL = 2048
BATCH = 4
SEQ = 2048
DEPTH = 4

GRID_W = 64
CTX_LEN = 256
HEAD_DIM = 64
D_MIX = D_MODEL
N_GROUPS = 4
GROUP_W = D_MIX // N_GROUPS
LRU_HEADS = GROUP_W // HEAD_DIM
LRU_BLOCK = GROUP_W // LRU_HEADS
LRU_CONV = 4
LRU_C = 8.0
MLA_HEADS = 8
MLA_NOPE = 64
MLA_ROPE = 32
MLA_V = GROUP_W // MLA_HEADS
MLA_Q_RANK = 384
MLA_KV_RANK = 256
DIFF_HEADS = 4
DIFF_D = 64
DIFF_V = GROUP_W // DIFF_HEADS
GQA_Q_HEADS = GROUP_W // HEAD_DIM
GQA_KV_HEADS = 2
GQA_GROUP = GQA_Q_HEADS // GQA_KV_HEADS
WINDOW = 128
D_FF = 5632
MACARON_WEIGHT = 0.5
ROPE_BASE = 10000.0
Q_BLOCK = 128
RMS_EPS = 1e-6
NEG_INF = -1e30
N_MOD = 9

COLS_LRU = 2 * GROUP_W
COLS_MLA = MLA_Q_RANK + MLA_KV_RANK + MLA_ROPE
COLS_DIFF = 2 * DIFF_HEADS * 2 * DIFF_D + DIFF_HEADS * DIFF_V
COLS_GQA = (GQA_Q_HEADS + 2 * GQA_KV_HEADS) * HEAD_DIM
N_IN = COLS_LRU + COLS_MLA + COLS_DIFF + COLS_GQA
IN_SPLITS = (COLS_LRU, COLS_LRU + COLS_MLA, COLS_LRU + COLS_MLA + COLS_DIFF)

kernel_name = "hymba_parallel_mixer_diffusion_trunk"

F32 = jnp.float32


def rmsnorm(x, g):
    xf = x.astype(F32)
    y = xf * lax.rsqrt(jnp.mean(xf * xf, axis=-1, keepdims=True) + RMS_EPS)
    return (y * g.astype(F32)).astype(x.dtype)


def axial_rope_tables(rows, rot_dim):
    row = jnp.repeat(jnp.arange(rows, dtype=F32), GRID_W)
    col = jnp.tile(jnp.arange(GRID_W, dtype=F32), rows)
    per_axis = rot_dim // 2
    inv = ROPE_BASE ** (-jnp.arange(0, per_axis, 2, dtype=F32) / per_axis)
    ang = jnp.concatenate([row[:, None] * inv, col[:, None] * inv], axis=-1)
    return jnp.cos(ang), jnp.sin(ang)


def apply_rope(x, cos, sin):
    half = x.shape[-1] // 2
    xf = x.astype(F32)
    x1, x2 = xf[..., :half], xf[..., half:]
    c = cos[None, :, None, :]
    s = sin[None, :, None, :]
    return jnp.concatenate([x1 * c - x2 * s, x1 * s + x2 * c], axis=-1).astype(x.dtype)


def modulation(cvec, w_mod, b_mod):
    m = jax.nn.silu(cvec) @ w_mod + b_mod
    return jnp.split(m, N_MOD, axis=-1)


def sublayer_in(x, g_pre, shift, scale):
    return rmsnorm(x, g_pre) * (1 + scale) + shift


def sublayer_out(x, y, g_post, gate, weight):
    return x + weight * gate * rmsnorm(y, g_post)


def swiglu(h, w_in, w_out):
    g, u = jnp.split(h @ w_in, 2, axis=-1)
    return (jax.nn.silu(g) * u) @ w_out


def ffn_sublayer(x, mods, g_pre, g_post, w_in, w_out):
    shift, scale, gate = mods
    h = sublayer_in(x, g_pre, shift, scale)
    return sublayer_out(x, swiglu(h, w_in, w_out), g_post, gate, MACARON_WEIGHT)


def merge_heads(y):
    return y.reshape(y.shape[:2] + (GROUP_W,))


def blockwise(fn, q):
    bsz, n = q.shape[:2]
    nb = n // Q_BLOCK
    qb = jnp.moveaxis(q.reshape((bsz, nb, Q_BLOCK) + q.shape[2:]), 1, 0)
    out = jnp.moveaxis(lax.map(fn, qb), 0, 1)
    return out.reshape((bsz, n) + out.shape[3:])


def softmax_attend(q, k, v, scale):
    s = jnp.einsum('bqhd,bkhd->bhqk', q, k).astype(F32) * scale
    p = jax.nn.softmax(s, axis=-1).astype(v.dtype)
    return jnp.einsum('bhqk,bkhe->bqhe', p, v)


def diff_attend(q, k, v, lam, scale):
    s = jnp.einsum('bqhjd,bkhjd->bhjqk', q, k).astype(F32) * scale
    p = jax.nn.softmax(s, axis=-1)
    w = (p[:, :, 0] - lam * p[:, :, 1]).astype(v.dtype)
    return jnp.einsum('bhqk,bkhe->bqhe', w, v)


def sink_softmax(s, sink):
    col = jnp.broadcast_to(sink, s.shape[:-1] + (1,))
    return jax.nn.softmax(jnp.concatenate([col, s], axis=-1), axis=-1)[..., 1:]


def depthwise_conv_centred(u, w, b):
    y = lax.conv_general_dilated(
        u, w[:, None, :].astype(u.dtype), window_strides=(1,),
        padding=[(LRU_CONV // 2, LRU_CONV - 1 - LRU_CONV // 2)],
        dimension_numbers=('NWC', 'WIO', 'NWC'), feature_group_count=u.shape[-1])
    return y + b.astype(u.dtype)


def rglru_coeffs(u, w_g, b_g, lam):
    bsz, n, ch = u.shape
    ub = u.reshape(bsz, n, LRU_HEADS, LRU_BLOCK)
    g = jnp.einsum('bnhi,ghij->gbnhj', ub, w_g.astype(F32)).reshape(2, bsz, n, ch)
    g = g + b_g.astype(F32)[:, None, None, :]
    r = jax.nn.sigmoid(g[0])
    i = jax.nn.sigmoid(g[1])
    log_a = -LRU_C * r * jax.nn.softplus(-lam.astype(F32))
    a = jnp.exp(log_a)
    return a, jnp.sqrt(-jnp.expm1(2.0 * log_a)) * (i * u)


def linear_scan(a, b, reverse):
    def combine(p, q):
        return p[0] * q[0], q[0] * p[1] + q[1]
    return lax.associative_scan(combine, (a, b), reverse=reverse, axis=1)[1]


def rglru_mixer(cols_lat, cols_ctx, conv_w, conv_b, w_gates, b_gates, lam, ctx_out):
    xb_l, gate_l = jnp.split(cols_lat, 2, axis=-1)
    xb_c, gate_c = jnp.split(cols_ctx, 2, axis=-1)
    u_l = depthwise_conv_centred(xb_l, conv_w, conv_b).astype(F32)
    u_c = depthwise_conv_centred(xb_c, conv_w, conv_b).astype(F32)
    h_lat, h_ctx = [], []
    for d, rev in enumerate((False, True)):
        a_c, b_c = rglru_coeffs(u_c, w_gates[d], b_gates[d], lam[d])
        hc = linear_scan(a_c, b_c, rev)
        h0 = hc[:, 0] if rev else hc[:, -1]
        a_l, b_l = rglru_coeffs(u_l, w_gates[d], b_gates[d], lam[d])
        first = -1 if rev else 0
        b_l = b_l.at[:, first].add(a_l[:, first] * h0)
        h_lat.append(linear_scan(a_l, b_l, rev))
        h_ctx.append(hc)
    y_lat = ((h_lat[0] + h_lat[1]) * jax.nn.gelu(gate_l.astype(F32))).astype(cols_lat.dtype)
    y_ctx = None
    if ctx_out:
        y_ctx = ((h_ctx[0] + h_ctx[1]) * jax.nn.gelu(gate_c.astype(F32))).astype(cols_ctx.dtype)
    return y_lat, y_ctx


def mla_queries(cols, q_norm, w_qb, rope):
    bsz, n, _ = cols.shape
    q = (rmsnorm(cols[..., :MLA_Q_RANK], q_norm) @ w_qb).reshape(bsz, n, MLA_HEADS, MLA_NOPE + MLA_ROPE)
    if rope is None:
        return q
    return jnp.concatenate([q[..., :MLA_NOPE], apply_rope(q[..., MLA_NOPE:], *rope)], axis=-1)


def mla_keys_values(cols, kv_norm, w_kvb, rope):
    bsz, n, _ = cols.shape
    ckv = cols[..., MLA_Q_RANK:MLA_Q_RANK + MLA_KV_RANK]
    k_rope = cols[..., MLA_Q_RANK + MLA_KV_RANK:][:, :, None, :]
    if rope is not None:
        k_rope = apply_rope(k_rope, *rope)
    kv = (rmsnorm(ckv, kv_norm) @ w_kvb).reshape(bsz, n, MLA_HEADS, MLA_NOPE + MLA_V)
    k = jnp.concatenate([kv[..., :MLA_NOPE],
                         jnp.broadcast_to(k_rope, (bsz, n, MLA_HEADS, MLA_ROPE))], axis=-1)
    return k, kv[..., MLA_NOPE:]


def mla_mixer(cols_lat, cols_ctx, q_norm, w_qb, kv_norm, w_kvb, rope, ctx_out):
    scale = (MLA_NOPE + MLA_ROPE) ** -0.5
    k_c, v_c = mla_keys_values(cols_ctx, kv_norm, w_kvb, None)
    k_l, v_l = mla_keys_values(cols_lat, kv_norm, w_kvb, rope)
    k_all = jnp.concatenate([k_c, k_l], axis=1)
    v_all = jnp.concatenate([v_c, v_l], axis=1)
    q_l = mla_queries(cols_lat, q_norm, w_qb, rope)
    y_lat = merge_heads(blockwise(lambda qi: softmax_attend(qi, k_all, v_all, scale), q_l))
    y_ctx = None
    if ctx_out:
        q_c = mla_queries(cols_ctx, q_norm, w_qb, None)
        y_ctx = merge_heads(softmax_attend(q_c, k_c, v_c, scale))
    return y_lat, y_ctx


def diff_project(cols, rope):
    bsz, n, _ = cols.shape
    qk = DIFF_HEADS * 2 * DIFF_D
    q = cols[..., :qk].reshape(bsz, n, 2 * DIFF_HEADS, DIFF_D)
    k = cols[..., qk:2 * qk].reshape(bsz, n, 2 * DIFF_HEADS, DIFF_D)
    v = cols[..., 2 * qk:].reshape(bsz, n, DIFF_HEADS, DIFF_V)
    if rope is not None:
        q = apply_rope(q, *rope)
        k = apply_rope(k, *rope)
    shape5 = (bsz, n, DIFF_HEADS, 2, DIFF_D)
    return q.reshape(shape5), k.reshape(shape5), v


def diff_mixer(cols_lat, cols_ctx, lam_vecs, norm_g, lam_init, rope, ctx_out):
    lv = lam_vecs.astype(F32)
    lam = jnp.exp(jnp.sum(lv[0] * lv[1])) - jnp.exp(jnp.sum(lv[2] * lv[3])) + lam_init
    scale = DIFF_D ** -0.5
    q_l, k_l, v_l = diff_project(cols_lat, rope)
    q_c, k_c, v_c = diff_project(cols_ctx, None)
    k_all = jnp.concatenate([k_c, k_l], axis=1)
    v_all = jnp.concatenate([v_c, v_l], axis=1)

    def finish(o):
        return merge_heads(rmsnorm(o, norm_g) * (1.0 - lam_init))

    y_lat = finish(blockwise(lambda qi: diff_attend(qi, k_all, v_all, lam, scale), q_l))
    y_ctx = finish(diff_attend(q_c, k_c, v_c, lam, scale)) if ctx_out else None
    return y_lat, y_ctx


def gqa_project(cols, rope):
    bsz, n, _ = cols.shape
    nq = GQA_Q_HEADS * HEAD_DIM
    nk = GQA_KV_HEADS * HEAD_DIM
    q = cols[..., :nq].reshape(bsz, n, GQA_Q_HEADS, HEAD_DIM)
    k = cols[..., nq:nq + nk].reshape(bsz, n, GQA_KV_HEADS, HEAD_DIM)
    v = cols[..., nq + nk:].reshape(bsz, n, GQA_KV_HEADS, HEAD_DIM)
    if rope is not None:
        q = apply_rope(q, *rope)
        k = apply_rope(k, *rope)
    return q.reshape(bsz, n, GQA_KV_HEADS, GQA_GROUP, HEAD_DIM), k, v


def gqa_window_attend(q, k, v, kc, vc, sink, scale):
    bsz, n, hk, grp, d = q.shape
    nb = n // WINDOW
    qb = q.reshape(bsz, nb, WINDOW, hk, grp, d)

    def band(t):
        tp = jnp.pad(t, ((0, 0), (WINDOW, WINDOW), (0, 0), (0, 0)))
        tp = tp.reshape(bsz, nb + 2, WINDOW, hk, t.shape[-1])
        return jnp.concatenate([tp[:, :-2], tp[:, 1:-1], tp[:, 2:]], axis=2)

    kb, vb = band(k), band(v)
    s_loc = jnp.einsum('bnqhgd,bnkhd->bnhgqk', qb, kb).astype(F32) * scale
    qpos = jnp.arange(nb)[:, None] * WINDOW + jnp.arange(WINDOW)[None, :]
    kpos = (jnp.arange(nb)[:, None] - 1) * WINDOW + jnp.arange(3 * WINDOW)[None, :]
    rel = kpos[:, None, :] - qpos[:, :, None]
    valid = (jnp.abs(rel) <= WINDOW) & (kpos[:, None, :] >= 0) & (kpos[:, None, :] < n)
    s_loc = jnp.where(valid[None, :, None, None], s_loc, NEG_INF)
    s_ctx = jnp.einsum('bnqhgd,bkhd->bnhgqk', qb, kc).astype(F32) * scale
    m = kc.shape[1]
    p = sink_softmax(jnp.concatenate([s_ctx, s_loc], axis=-1), sink)
    p_ctx = p[..., :m].astype(v.dtype)
    p_loc = p[..., m:].astype(v.dtype)
    o = (jnp.einsum('bnhgqk,bkhd->bnqhgd', p_ctx, vc)
         + jnp.einsum('bnhgqk,bnkhd->bnqhgd', p_loc, vb))
    return o.reshape(bsz, n, GROUP_W)


def gqa_mixer(cols_lat, cols_ctx, sink_logits, rope, ctx_out):
    scale = HEAD_DIM ** -0.5
    sink = sink_logits.astype(F32).reshape(GQA_KV_HEADS, GQA_GROUP)[:, :, None, None]
    q_l, k_l, v_l = gqa_project(cols_lat, rope)
    q_c, k_c, v_c = gqa_project(cols_ctx, None)
    y_lat = gqa_window_attend(q_l, k_l, v_l, k_c, v_c, sink, scale)
    y_ctx = None
    if ctx_out:
        s = jnp.einsum('bqhgd,bkhd->bhgqk', q_c, k_c).astype(F32) * scale
        p = sink_softmax(s, sink).astype(v_c.dtype)
        o = jnp.einsum('bhgqk,bkhd->bqhgd', p, v_c)
        y_ctx = o.reshape(o.shape[0], o.shape[1], GROUP_W)
    return y_lat, y_ctx


def parallel_mixers(h_lat, h_ctx, w_in, w_out, lru_conv_w, lru_conv_b, lru_w_gates, lru_b_gates,
                    lru_lambda, mla_q_norm, mla_w_qb, mla_kv_norm, mla_w_kvb, diff_lambda, diff_norm,
                    diff_lambda_init, gqa_sink, rope_hd, rope_mla, ctx_out):
    a_l, b_l, c_l, d_l = jnp.split(h_lat @ w_in, list(IN_SPLITS), axis=-1)
    a_c, b_c, c_c, d_c = jnp.split(h_ctx @ w_in, list(IN_SPLITS), axis=-1)
    ya = rglru_mixer(a_l, a_c, lru_conv_w, lru_conv_b, lru_w_gates, lru_b_gates, lru_lambda, ctx_out)
    yb = mla_mixer(b_l, b_c, mla_q_norm, mla_w_qb, mla_kv_norm, mla_w_kvb, rope_mla, ctx_out)
    yc = diff_mixer(c_l, c_c, diff_lambda, diff_norm, diff_lambda_init, rope_hd, ctx_out)
    yd = gqa_mixer(d_l, d_c, gqa_sink, rope_hd, ctx_out)
    y_lat = jnp.concatenate([ya[0], yb[0], yc[0], yd[0]], axis=-1) @ w_out
    y_ctx = None
    if ctx_out:
        y_ctx = jnp.concatenate([ya[1], yb[1], yc[1], yd[1]], axis=-1) @ w_out
    return y_lat, y_ctx


def setup_inputs(seed: int = 0) -> dict:
    key = jax.random.key(seed)
    ks = jax.random.split(key, 24)
    L, D = DEPTH, D_MODEL

    def nrm(k, shape, fan_in, gain=1.0):
        return jax.random.normal(k, shape, F32) * (gain * fan_in ** -0.5)

    def gain(k, shape):
        return 1.0 + 0.05 * jax.random.normal(k, shape, F32)

    u = jax.random.uniform(ks[14], (L, 2, GROUP_W), F32, 0.9, 0.999)
    s = u ** (1.0 / LRU_C)
    lru_lambda = jnp.log(s) - jnp.log1p(-s)
    return {
        "x": jax.random.normal(ks[0], (BATCH, SEQ, D), F32),
        "c": jax.random.normal(ks[1], (BATCH, D), F32),
        "ctx": jax.random.normal(ks[2], (BATCH, CTX_LEN, D), F32),
        "c_ctx": jax.random.normal(ks[3], (D,), F32),
        "w_mod": nrm(ks[4], (L, D, N_MOD * D), D, 0.5),
        "b_mod": 0.02 * jax.random.normal(ks[5], (L, N_MOD * D), F32),
        "norm_g": gain(ks[6], (L, 6, D)),
        "ffn_w_in": nrm(ks[7], (L, 2, D, 2 * D_FF), D),
        "ffn_w_out": nrm(ks[8], (L, 2, D_FF, D), D_FF),
        "w_in": nrm(ks[9], (L, D, N_IN), D),
        "w_out": nrm(ks[10], (L, D_MIX, D), D_MIX),
        "lru_conv_w": nrm(ks[11], (L, LRU_CONV, GROUP_W), LRU_CONV),
        "lru_conv_b": 0.02 * jax.random.normal(ks[12], (L, GROUP_W), F32),
        "lru_w_gates": nrm(ks[13], (L, 2, 2, LRU_HEADS, LRU_BLOCK, LRU_BLOCK), LRU_BLOCK),
        "lru_b_gates": 0.02 * jax.random.normal(ks[15], (L, 2, 2, GROUP_W), F32),
        "lru_lambda": lru_lambda,
        "mla_q_norm": gain(ks[16], (L, MLA_Q_RANK)),
        "mla_w_qb": nrm(ks[17], (L, MLA_Q_RANK, MLA_HEADS * (MLA_NOPE + MLA_ROPE)), MLA_Q_RANK),
        "mla_kv_norm": gain(ks[18], (L, MLA_KV_RANK)),
        "mla_w_kvb": nrm(ks[19], (L, MLA_KV_RANK, MLA_HEADS * (MLA_NOPE + MLA_V)), MLA_KV_RANK),
        "diff_lambda": 0.1 * jax.random.normal(ks[20], (L, 4, DIFF_D), F32),
        "diff_norm": gain(ks[21], (L, DIFF_V)),
        "gqa_sink": 0.5 * jax.random.normal(ks[22], (L, GQA_Q_HEADS), F32),
    }


def reference(x, c, ctx, c_ctx, w_mod, b_mod, norm_g, ffn_w_in, ffn_w_out, w_in, w_out,
              lru_conv_w, lru_conv_b, lru_w_gates, lru_b_gates, lru_lambda,
              mla_q_norm, mla_w_qb, mla_kv_norm, mla_w_kvb, diff_lambda, diff_norm, gqa_sink):
    n = x.shape[1]
    rows = n // GRID_W
    rope_hd = axial_rope_tables(rows, HEAD_DIM)
    rope_mla = axial_rope_tables(rows, MLA_ROPE)
    x_lat, x_ctx = x, ctx
    for layer in range(DEPTH):
        last = layer == DEPTH - 1
        mod_l = modulation(c[:, None, :], w_mod[layer], b_mod[layer])
        mod_c = modulation(c_ctx[None, None, :], w_mod[layer], b_mod[layer])
        g = norm_g[layer]
        lam_init = 0.8 - 0.6 * math.exp(-0.3 * layer)
        x_lat = ffn_sublayer(x_lat, mod_l[0:3], g[0], g[1], ffn_w_in[layer, 0], ffn_w_out[layer, 0])
        x_ctx = ffn_sublayer(x_ctx, mod_c[0:3], g[0], g[1], ffn_w_in[layer, 0], ffn_w_out[layer, 0])
        h_lat = sublayer_in(x_lat, g[2], mod_l[3], mod_l[4])
        h_ctx = sublayer_in(x_ctx, g[2], mod_c[3], mod_c[4])
        y_lat, y_ctx = parallel_mixers(
            h_lat, h_ctx, w_in[layer], w_out[layer], lru_conv_w[layer], lru_conv_b[layer],
            lru_w_gates[layer], lru_b_gates[layer], lru_lambda[layer], mla_q_norm[layer],
            mla_w_qb[layer], mla_kv_norm[layer], mla_w_kvb[layer], diff_lambda[layer],
            diff_norm[layer], lam_init, gqa_sink[layer], rope_hd, rope_mla, not last)
        x_lat = sublayer_out(x_lat, y_lat, g[3], mod_l[5], 1.0)
        if not last:
            x_ctx = sublayer_out(x_ctx, y_ctx, g[3], mod_c[5], 1.0)
            x_ctx = ffn_sublayer(x_ctx, mod_c[6:9], g[4], g[5], ffn_w_in[layer, 1], ffn_w_out[layer, 1])
        x_lat = ffn_sublayer(x_lat, mod_l[6:9], g[4], g[5], ffn_w_in[layer, 1], ffn_w_out[layer, 1])
    return x_lat
```

```python
import functools
import math

import jax
import jax.numpy as jnp
from jax import lax
from jax.experimental import pallas as pl
from jax.experimental.pallas import tpu as pltpu

F32 = jnp.float32
BF16 = jnp.bfloat16

GRID_W = 64
HEAD_DIM = 64
N_MOD = 9
GROUP_W = 512
LRU_HEADS = 8
LRU_BLOCK = 64
LRU_C = 8.0
MLA_HEADS = 8
MLA_NOPE = 64
MLA_ROPE = 32
MLA_V = 64
MLA_Q_RANK = 384
MLA_KV_RANK = 256
DIFF_HEADS = 4
DIFF_D = 64
DIFF_V = 128
GQA_Q_HEADS = 8
GQA_KV_HEADS = 2
GQA_GROUP = 4
WINDOW = 128
MACARON_WEIGHT = 0.5
ROPE_BASE = 10000.0
RMS_EPS = 1e-6
NEG_INF = -1e30

LANES = 128
SUBLANES = 8
MLA_PAD = 128

COLS_LRU = 2 * GROUP_W
COLS_MLA = MLA_Q_RANK + MLA_KV_RANK + MLA_ROPE
COLS_DIFF = 2 * DIFF_HEADS * 2 * DIFF_D + DIFF_HEADS * DIFF_V
COLS_GQA = (GQA_Q_HEADS + 2 * GQA_KV_HEADS) * HEAD_DIM
P_LRU = 0
P_MLA = 1024
W_MLA = 768
P_DIFF = P_MLA + W_MLA
P_GQA = P_DIFF + COLS_DIFF
N_IN_P = P_GQA + COLS_GQA

VMEM_LIMIT = 60 * 1024 * 1024


def _cparams(sem):
    return pltpu.CompilerParams(dimension_semantics=sem, vmem_limit_bytes=VMEM_LIMIT)


def _rms(x, g):
    ms = jnp.mean(x * x, axis=-1, keepdims=True)
    return x * lax.rsqrt(ms + RMS_EPS) * g


def _dot(a, b):
    return jnp.dot(a, b, preferred_element_type=F32)


def _dot_nt(a, b):
    return lax.dot_general(a, b, (((1,), (1,)), ((), ())), preferred_element_type=F32)


def _mod_kernel(c_ref, w_ref, b_ref, o_ref):
    s = c_ref[...]
    s = s * jax.nn.sigmoid(s)
    o_ref[...] = _dot(s.astype(BF16), w_ref[...].astype(BF16)) + b_ref[...]


def _modulation(cvec, w_mod, b_mod, tn):
    depth, d, n = w_mod.shape
    rows = cvec.shape[0]
    return pl.pallas_call(
        _mod_kernel,
        grid=(depth, n // tn),
        in_specs=[
            pl.BlockSpec((rows, d), lambda l, j: (0, 0)),
            pl.BlockSpec((None, d, tn), lambda l, j: (l, 0, j)),
            pl.BlockSpec((None, 1, tn), lambda l, j: (l, 0, j)),
        ],
        out_specs=pl.BlockSpec((None, rows, tn), lambda l, j: (l, 0, j)),
        out_shape=jax.ShapeDtypeStruct((depth, rows, n), F32),
        compiler_params=_cparams(("arbitrary", "arbitrary")),
        name="modulation",
    )(cvec, w_mod, b_mod.reshape(depth, 1, n))


def _ffn_kernel(x_ref, shift_ref, scale_ref, gate_ref, gpre_ref, gpost_ref,
                wg_ref, wu_ref, wo_ref, o_ref, h_ref, *, nj):
    j = pl.program_id(1)

    @pl.when(j == 0)
    def _():
        h = _rms(x_ref[...], gpre_ref[...]) * (1.0 + scale_ref[...]) + shift_ref[...]
        h_ref[...] = h.astype(BF16)
        o_ref[...] = jnp.zeros_like(o_ref)

    h = h_ref[...]
    g = _dot(h, wg_ref[...].astype(BF16))
    u = _dot(h, wu_ref[...].astype(BF16))
    a = (g * jax.nn.sigmoid(g) * u).astype(BF16)
    o_ref[...] += _dot(a, wo_ref[...].astype(BF16))

    @pl.when(j == nj - 1)
    def _():
        o_ref[...] = x_ref[...] + (MACARON_WEIGHT * gate_ref[...]) * _rms(o_ref[...], gpost_ref[...])


def _mod_spec(d, row_fn, k):
    return pl.BlockSpec((None, 1, d), lambda i, *_: (row_fn(i), 0, k))


def _ffn_sublayer(xs, n_rows, mod3, k0, g_pre, g_post, w_in_all, w_out_all, layer, sub,
                  row_fn, tm, tf):
    d = xs.shape[1]
    d_ff = w_out_all.shape[2]
    nj = d_ff // tf
    return pl.pallas_call(
        functools.partial(_ffn_kernel, nj=nj),
        grid=(n_rows // tm, nj),
        in_specs=[
            pl.BlockSpec((tm, d), lambda i, j: (i, 0), pipeline_mode=pl.Buffered(1)),
            _mod_spec(d, row_fn, k0), _mod_spec(d, row_fn, k0 + 1), _mod_spec(d, row_fn, k0 + 2),
            pl.BlockSpec((1, d), lambda i, j: (0, 0)),
            pl.BlockSpec((1, d), lambda i, j: (0, 0)),
            pl.BlockSpec((None, None, d, tf), lambda i, j: (layer, sub, 0, j)),
            pl.BlockSpec((None, None, d, tf), lambda i, j: (layer, sub, 0, nj + j)),
            pl.BlockSpec((None, None, tf, d), lambda i, j: (layer, sub, j, 0)),
        ],
        out_specs=pl.BlockSpec((tm, d), lambda i, j: (i, 0)),
        out_shape=jax.ShapeDtypeStruct((n_rows, d), F32),
        scratch_shapes=[pltpu.VMEM((tm, d), BF16)],
        compiler_params=_cparams(("arbitrary", "arbitrary")),
        name="ffn_sublayer",
    )(xs, mod3, mod3, mod3, g_pre, g_post, w_in_all, w_in_all, w_out_all)


def _swap_halves(x, first, half):
    return jnp.where(first, pltpu.roll(x, LANES - half, 1), pltpu.roll(x, half, 1))


def _rope_chunks(x, cos, sin, first, half):
    out = []
    for c in range(x.shape[1] // LANES):
        blk = x[:, c * LANES:(c + 1) * LANES]
        out.append(blk * cos + _swap_halves(blk, first, half) * sin)
    return out[0] if len(out) == 1 else jnp.concatenate(out, axis=1)


def _mixer_in_kernel(x_ref, shift_ref, scale_ref, g_ref, w_ref,
                     cos_hd_ref, sin_hd_ref, cos_mla_ref, sin_mla_ref,
                     qn_ref, wq_ref, kvn_ref, wk_ref, wv_ref,
                     lru_ref, mq_ref, mk_ref, mv_ref, dq_ref, dk_ref, dv_ref,
                     gq_ref, gk_ref, gv_ref):
    h = (_rms(x_ref[...], g_ref[...]) * (1.0 + scale_ref[...]) + shift_ref[...]).astype(BF16)
    tm = h.shape[0]
    lane = lax.broadcasted_iota(jnp.int32, (tm, LANES), 1)
    first_hd = (lane % HEAD_DIM) < (HEAD_DIM // 2)
    first_mla = lane < (MLA_NOPE + MLA_ROPE // 2)
    cos_hd, sin_hd = cos_hd_ref[...], sin_hd_ref[...]
    cos_mla, sin_mla = cos_mla_ref[...], sin_mla_ref[...]

    lru_ref[...] = _dot(h, w_ref[:, P_LRU:P_LRU + COLS_LRU])

    cm = _dot(h, w_ref[:, P_MLA:P_MLA + W_MLA])
    qn = _rms(cm[:, :MLA_Q_RANK], qn_ref[...]).astype(BF16)
    q = _rope_chunks(_dot(qn, wq_ref[...]), cos_mla, sin_mla, first_mla, MLA_ROPE // 2)
    mq_ref[...] = (q * (MLA_NOPE + MLA_ROPE) ** -0.5).astype(BF16)
    kvn = _rms(cm[:, MLA_Q_RANK:MLA_Q_RANK + MLA_KV_RANK], kvn_ref[...]).astype(BF16)
    k_rope = _rope_chunks(cm[:, W_MLA - LANES:], cos_mla, sin_mla, first_mla, MLA_ROPE // 2)
    k_nope = _dot(kvn, wk_ref[...])
    mk_ref[...] = (k_nope + jnp.concatenate([k_rope] * MLA_HEADS, axis=1)).astype(BF16)
    mv_ref[...] = _dot(kvn, wv_ref[...]).astype(BF16)

    qk = DIFF_HEADS * 2 * DIFF_D
    cd = _dot(h, w_ref[:, P_DIFF:P_DIFF + COLS_DIFF])
    dq_ref[...] = (_rope_chunks(cd[:, :qk], cos_hd, sin_hd, first_hd, HEAD_DIM // 2)
                   * DIFF_D ** -0.5).astype(BF16)
    dk_ref[...] = _rope_chunks(cd[:, qk:2 * qk], cos_hd, sin_hd, first_hd, HEAD_DIM // 2).astype(BF16)
    dv_ref[...] = cd[:, 2 * qk:].astype(BF16)

    nq = GQA_Q_HEADS * HEAD_DIM
    nk = GQA_KV_HEADS * HEAD_DIM
    cg = _dot(h, w_ref[:, P_GQA:P_GQA + COLS_GQA])
    gq_ref[...] = (_rope_chunks(cg[:, :nq], cos_hd, sin_hd, first_hd, HEAD_DIM // 2)
                   * HEAD_DIM ** -0.5).astype(BF16)
    gk_ref[...] = _rope_chunks(cg[:, nq:nq + nk], cos_hd, sin_hd, first_hd, HEAD_DIM // 2).astype(BF16)
    gv_ref[...] = cg[:, nq + nk:].astype(BF16)


def _const_spec(shape):
    nd = len(shape)
    return pl.BlockSpec(shape, lambda *_: (0,) * nd, pipeline_mode=pl.Buffered(1))


def _mixer_in(xs, mod3, g, w_in_p, tabs, mla_w, row_fn, tab_fn, tm):
    n_rows, d = xs.shape
    cos_hd, sin_hd, cos_mla, sin_mla = tabs
    qn, wq, kvn, wk, wv = mla_w
    tab_spec = pl.BlockSpec((tm, LANES), lambda i: (tab_fn(i), 0))
    widths = [(COLS_LRU, F32), (MLA_HEADS * MLA_PAD, BF16), (MLA_HEADS * MLA_PAD, BF16),
              (GROUP_W, BF16), (GROUP_W, BF16), (GROUP_W, BF16), (GROUP_W, BF16),
              (GROUP_W, BF16), (GQA_KV_HEADS * HEAD_DIM, BF16), (GQA_KV_HEADS * HEAD_DIM, BF16)]
    return pl.pallas_call(
        _mixer_in_kernel,
        grid=(n_rows // tm,),
        in_specs=[
            pl.BlockSpec((tm, d), lambda i: (i, 0)),
            _mod_spec(d, row_fn, 3), _mod_spec(d, row_fn, 4),
            _const_spec((1, d)), _const_spec(w_in_p.shape),
            tab_spec, tab_spec, tab_spec, tab_spec,
            _const_spec(qn.shape), _const_spec(wq.shape), _const_spec(kvn.shape),
            _const_spec(wk.shape), _const_spec(wv.shape),
        ],
        out_specs=[pl.BlockSpec((tm, w), lambda i: (i, 0)) for w, _ in widths],
        out_shape=[jax.ShapeDtypeStruct((n_rows, w), dt) for w, dt in widths],
        compiler_params=_cparams(("arbitrary",)),
        name="mixer_in",
    )(xs, mod3, mod3, g, w_in_p, cos_hd, sin_hd, cos_mla, sin_mla, qn, wq, kvn, wk, wv)


def _lru_direction(x, prev, nxt, cw, cb, wg, bg, lam, reverse, carry_ref, h_ref, a_scr, b_scr):
    tc, ch = x.shape
    ext = jnp.concatenate([prev, x, nxt], axis=0)
    n_ext = tc + 2 * SUBLANES

    def tap(shift):
        if shift == 0:
            return x
        return pltpu.roll(ext, (-shift) % n_ext, 0)[SUBLANES:SUBLANES + tc]

    u = cw[0:1] * tap(-2) + cw[1:2] * tap(-1) + cw[2:3] * x + cw[3:4] * tap(1) + cb
    g = _dot(u.astype(BF16), wg) + bg
    r = jax.nn.sigmoid(g[:, :ch])
    i = jax.nn.sigmoid(g[:, ch:])
    log_a = -LRU_C * r * jax.nn.softplus(-lam)
    a = jnp.exp(log_a)
    b = jnp.sqrt((1.0 - a) * (1.0 + a)) * (i * u)

    row = lax.broadcasted_iota(jnp.int32, (tc, ch), 0) % SUBLANES
    for s in (1, 2, 4):
        if reverse:
            ok = row < SUBLANES - s
            a_sh, b_sh = pltpu.roll(a, tc - s, 0), pltpu.roll(b, tc - s, 0)
        else:
            ok = row >= s
            a_sh, b_sh = pltpu.roll(a, s, 0), pltpu.roll(b, s, 0)
        b = jnp.where(ok, a * b_sh + b, b)
        a = jnp.where(ok, a * a_sh, a)
    a_scr[...] = a
    b_scr[...] = b

    nt = tc // SUBLANES

    def body(k, carry):
        kk = nt - 1 - k if reverse else k
        sl = pl.ds(pl.multiple_of(kk * SUBLANES, SUBLANES), SUBLANES)
        ht = b_scr[sl, :] + a_scr[sl, :] * carry
        h_ref[sl, :] = ht
        edge = ht[0:1, :] if reverse else ht[SUBLANES - 1:SUBLANES, :]
        return jnp.broadcast_to(edge, (SUBLANES, ch))

    carry_ref[...] = lax.fori_loop(0, nt, body, carry_ref[...])


def _lru_kernel(xf_ref, pf_ref, nf_ref, xb_ref, pb_ref, nb_ref,
                cw_ref, cb_ref, wgf_ref, wgb_ref, bgf_ref, bgb_ref, lamf_ref, lamb_ref,
                hf_ref, hb_ref, cf_ref, cbk_ref, a_scr, b_scr, *, halo_fn):
    s = pl.program_id(1)

    @pl.when(s == 0)
    def _():
        cf_ref[...] = jnp.zeros_like(cf_ref)
        cbk_ref[...] = jnp.zeros_like(cbk_ref)

    pvf, nvf, pvb, nvb = halo_fn(s)
    cw, cb = cw_ref[...], cb_ref[...]
    _lru_direction(xf_ref[...], pf_ref[...] * pvf, nf_ref[...] * nvf, cw, cb,
                   wgf_ref[...], bgf_ref[...], lamf_ref[...], False, cf_ref, hf_ref, a_scr, b_scr)
    _lru_direction(xb_ref[...], pb_ref[...] * pvb, nb_ref[...] * nvb, cw, cb,
                   wgb_ref[...], bgb_ref[...], lamb_ref[...], True, cbk_ref, hb_ref, a_scr, b_scr)


def _lru_mixer(lru_cols, conv_w, conv_b, wg, bg, lam, batch, seq, ctx_len, tc):
    n_rows = lru_cols.shape[0]
    ch = GROUP_W
    nl, ncx = seq // tc, ctx_len // tc
    n_steps = nl + ncx
    r8 = tc // SUBLANES
    last8 = n_rows // SUBLANES - 1

    def chunk_f(b, s):
        return jnp.where(s < ncx, batch * nl + b * ncx + s, b * nl + s - ncx)

    def chunk_b(b, s):
        return jnp.where(s < ncx, batch * nl + b * ncx + ncx - 1 - s, b * nl + nl - 1 - (s - ncx))

    def seg_pos(s, reverse):
        in_ctx = s < ncx
        if reverse:
            idx = jnp.where(in_ctx, ncx - 1 - s, nl - 1 - (s - ncx))
        else:
            idx = jnp.where(in_ctx, s, s - ncx)
        return idx, jnp.where(in_ctx, ncx, nl)

    def halo_fn(s):
        out = []
        for reverse in (False, True):
            idx, n = seg_pos(s, reverse)
            out += [(idx > 0).astype(F32), (idx < n - 1).astype(F32)]
        return out

    def x_spec(fn):
        return pl.BlockSpec((tc, ch), lambda b, s: (fn(b, s), 0))

    def prev_spec(fn):
        return pl.BlockSpec((SUBLANES, ch), lambda b, s: (jnp.maximum(fn(b, s) * r8 - 1, 0), 0))

    def next_spec(fn):
        return pl.BlockSpec((SUBLANES, ch), lambda b, s: (jnp.minimum((fn(b, s) + 1) * r8, last8), 0))

    def c_spec(shape):
        return pl.BlockSpec(shape, lambda b, s: (0,) * len(shape))

    return pl.pallas_call(
        functools.partial(_lru_kernel, halo_fn=halo_fn),
        grid=(batch, n_steps),
        in_specs=[
            x_spec(chunk_f), prev_spec(chunk_f), next_spec(chunk_f),
            x_spec(chunk_b), prev_spec(chunk_b), next_spec(chunk_b),
            c_spec(conv_w.shape), c_spec(conv_b.shape),
            c_spec(wg[0].shape), c_spec(wg[1].shape), c_spec(bg[0].shape), c_spec(bg[1].shape),
            c_spec(lam[0].shape), c_spec(lam[1].shape),
        ],
        out_specs=[x_spec(chunk_f), x_spec(chunk_b)],
        out_shape=[jax.ShapeDtypeStruct((n_rows, ch), F32)] * 2,
        scratch_shapes=[pltpu.VMEM((SUBLANES, ch), F32), pltpu.VMEM((SUBLANES, ch), F32),
                        pltpu.VMEM((tc, ch), F32), pltpu.VMEM((tc, ch), F32)],
        compiler_params=_cparams(("arbitrary", "arbitrary")),
        name="rglru",
    )(lru_cols, lru_cols, lru_cols, lru_cols, lru_cols, lru_cols,
      conv_w, conv_b, wg[0], wg[1], bg[0], bg[1], lam[0], lam[1])


def _softmax_parts(scores):
    m = scores[0].max(axis=-1, keepdims=True)
    for s in scores[1:]:
        m = jnp.maximum(m, s.max(axis=-1, keepdims=True))
    ps = [jnp.exp(s - m) for s in scores]
    den = ps[0].sum(axis=-1, keepdims=True)
    for p in ps[1:]:
        den = den + p.sum(axis=-1, keepdims=True)
    return ps, den


def _mla_kernel(q_ref, *refs, n_seg):
    ks, vs, o_ref = refs[:n_seg], refs[n_seg:2 * n_seg], refs[2 * n_seg]
    for h in range(MLA_HEADS):
        q = q_ref[:, h * MLA_PAD:(h + 1) * MLA_PAD]
        ps, den = _softmax_parts([_dot_nt(q, k[:, h * MLA_PAD:(h + 1) * MLA_PAD]) for k in ks])
        o = None
        for p, v in zip(ps, vs):
            t = _dot(p.astype(BF16), v[:, h * MLA_V:(h + 1) * MLA_V])
            o = t if o is None else o + t
        o_ref[:, h * MLA_V:(h + 1) * MLA_V] = (o / den).astype(o_ref.dtype)


def _diff_kernel(q_ref, lv_ref, g_ref, *refs, n_seg, lam_init):
    ks, vs, o_ref = refs[:n_seg], refs[n_seg:2 * n_seg], refs[2 * n_seg]
    lv = lv_ref[...]
    lam = (jnp.exp(jnp.sum(lv[0:1] * lv[1:2], axis=-1, keepdims=True))
           - jnp.exp(jnp.sum(lv[2:3] * lv[3:4], axis=-1, keepdims=True)) + lam_init)
    tq = q_ref.shape[0]
    lane = lax.broadcasted_iota(jnp.int32, (tq, 2 * DIFF_D), 1)
    zero = jnp.zeros((), BF16)
    for h in range(DIFF_HEADS):
        sl = slice(h * 2 * DIFF_D, (h + 1) * 2 * DIFF_D)
        qc = q_ref[:, sl]
        q1 = jnp.where(lane < DIFF_D, qc, zero)
        q2 = jnp.where(lane >= DIFF_D, qc, zero)
        p1, d1 = _softmax_parts([_dot_nt(q1, k[:, sl]) for k in ks])
        p2, d2 = _softmax_parts([_dot_nt(q2, k[:, sl]) for k in ks])
        c1 = 1.0 / d1
        c2 = lam / d2
        o = None
        for a, b, v in zip(p1, p2, vs):
            t = _dot((a * c1 - b * c2).astype(BF16), v[:, h * DIFF_V:(h + 1) * DIFF_V])
            o = t if o is None else o + t
        o_ref[:, h * DIFF_V:(h + 1) * DIFF_V] = (
            _rms(o, g_ref[...]) * (1.0 - lam_init)).astype(o_ref.dtype)


def _full_attention(kernel, q, k, v, extra, batch, seq, ctx_len, tq, latent, out_w):
    n_rows = q.shape[0]
    qw, kw, vw = q.shape[1], k.shape[1], v.shape[1]
    ctx_blk0 = batch * seq // ctx_len

    def ctx_spec(w):
        return pl.BlockSpec((ctx_len, w), lambda b, i: (ctx_blk0 + b, 0))

    def lat_spec(w):
        return pl.BlockSpec((seq, w), lambda b, i: (b, 0))

    if latent:
        nq = seq // tq
        q_map = lambda b, i: (b * nq + i, 0)
        kv_specs = [lat_spec(kw), ctx_spec(kw), lat_spec(vw), ctx_spec(vw)]
        kv_args = [k, k, v, v]
        n_seg = 2
    else:
        nq = ctx_len // tq
        q_map = lambda b, i: (batch * seq // tq + b * nq + i, 0)
        kv_specs = [ctx_spec(kw), ctx_spec(vw)]
        kv_args = [k, v]
        n_seg = 1
    extra_specs = [pl.BlockSpec(e.shape, lambda b, i, nd=e.ndim: (0,) * nd) for e in extra]
    return pl.pallas_call(
        functools.partial(kernel, n_seg=n_seg),
        grid=(batch, nq),
        in_specs=[pl.BlockSpec((tq, qw), q_map)] + extra_specs + kv_specs,
        out_specs=pl.BlockSpec((tq, out_w), q_map),
        out_shape=jax.ShapeDtypeStruct((n_rows, out_w), BF16),
        compiler_params=_cparams(("arbitrary", "arbitrary")),
        name="full_attention",
    )(q, *extra, *kv_args)


def _gqa_finish(scores, vals, sink, kv, o_ref, hq):
    m = jnp.maximum(scores[0].max(axis=-1, keepdims=True), sink)
    for s in scores[1:]:
        m = jnp.maximum(m, s.max(axis=-1, keepdims=True))
    den = jnp.exp(sink - m)
    o = None
    for s, v in zip(scores, vals):
        p = jnp.exp(s - m)
        den = den + p.sum(axis=-1, keepdims=True)
        t = _dot(p.astype(BF16), v[:, kv * HEAD_DIM:(kv + 1) * HEAD_DIM])
        o = t if o is None else o + t
    o_ref[:, hq * HEAD_DIM:(hq + 1) * HEAD_DIM] = (o / den).astype(o_ref.dtype)


def _gqa_lat_kernel(sink_ref, q_ref, kl_ref, vl_ref, kc_ref, vc_ref, o_ref, *, seq):
    n = pl.program_id(1)
    span = 3 * WINDOW
    start = pl.multiple_of(jnp.clip((n - 1) * WINDOW, 0, seq - span), WINDOW)
    kw = kl_ref[pl.ds(start, span), :]
    vw = vl_ref[pl.ds(start, span), :]
    kc, vc = kc_ref[...], vc_ref[...]
    qpos = n * WINDOW + lax.broadcasted_iota(jnp.int32, (WINDOW, span), 0)
    kpos = start + lax.broadcasted_iota(jnp.int32, (WINDOW, span), 1)
    valid = jnp.abs(kpos - qpos) <= WINDOW
    for hq in range(GQA_Q_HEADS):
        kv = hq // GQA_GROUP
        ksl = slice(kv * HEAD_DIM, (kv + 1) * HEAD_DIM)
        q = q_ref[:, hq * HEAD_DIM:(hq + 1) * HEAD_DIM]
        s_ctx = _dot_nt(q, kc[:, ksl])
        s_loc = jnp.where(valid, _dot_nt(q, kw[:, ksl]), NEG_INF)
        _gqa_finish([s_ctx, s_loc], [vc, vw], sink_ref[hq], kv, o_ref, hq)


def _gqa_ctx_kernel(sink_ref, q_ref, kc_ref, vc_ref, o_ref):
    kc, vc = kc_ref[...], vc_ref[...]
    for hq in range(GQA_Q_HEADS):
        kv = hq // GQA_GROUP
        q = q_ref[:, hq * HEAD_DIM:(hq + 1) * HEAD_DIM]
        s_ctx = _dot_nt(q, kc[:, kv * HEAD_DIM:(kv + 1) * HEAD_DIM])
        _gqa_finish([s_ctx], [vc], sink_ref[hq], kv, o_ref, hq)


def _gqa_attention(q, k, v, sink, batch, seq, ctx_len, latent):
    n_rows = q.shape[0]
    kvw = GQA_KV_HEADS * HEAD_DIM
    ctx_blk0 = batch * seq // ctx_len
    smem = pl.BlockSpec(memory_space=pltpu.SMEM)
    ctx_spec = pl.BlockSpec((ctx_len, kvw), lambda b, i: (ctx_blk0 + b, 0))
    if latent:
        nq = seq // WINDOW
        q_map = lambda b, i: (b * nq + i, 0)
        lat_spec = pl.BlockSpec((seq, kvw), lambda b, i: (b, 0))
        kernel = functools.partial(_gqa_lat_kernel, seq=seq)
        specs, args, tq = [lat_spec, lat_spec, ctx_spec, ctx_spec], [k, v, k, v], WINDOW
    else:
        nq = 1
        q_map = lambda b, i: (ctx_blk0 + b, 0)
        kernel, specs, args, tq = _gqa_ctx_kernel, [ctx_spec, ctx_spec], [k, v], ctx_len
    return pl.pallas_call(
        kernel,
        grid=(batch, nq),
        in_specs=[smem, pl.BlockSpec((tq, GROUP_W), q_map)] + specs,
        out_specs=pl.BlockSpec((tq, GROUP_W), q_map),
        out_shape=jax.ShapeDtypeStruct((n_rows, GROUP_W), BF16),
        compiler_params=_cparams(("arbitrary", "arbitrary")),
        name="gqa_attention",
    )(sink, q, *args)


def _mixer_out_kernel(x_ref, gate_ref, g_ref, hf_ref, hb_ref, lg_ref, yb_ref, yc_ref, yd_ref,
                      w_ref, o_ref):
    ya = ((hf_ref[...] + hb_ref[...]) * jax.nn.gelu(lg_ref[...])).astype(BF16)
    y = _dot(ya, w_ref[0:GROUP_W, :])
    y += _dot(yb_ref[...], w_ref[GROUP_W:2 * GROUP_W, :])
    y += _dot(yc_ref[...], w_ref[2 * GROUP_W:3 * GROUP_W, :])
    y += _dot(yd_ref[...], w_ref[3 * GROUP_W:4 * GROUP_W, :])
    o_ref[...] = x_ref[...] + gate_ref[...] * _rms(y, g_ref[...])


def _mixer_out(xs, n_rows, mod3, g, hf, hb, lru_cols, yb, yc, yd, w_out, row_fn, tm):
    d = xs.shape[1]
    row_spec = pl.BlockSpec((tm, GROUP_W), lambda i: (i, 0))
    return pl.pallas_call(
        _mixer_out_kernel,
        grid=(n_rows // tm,),
        in_specs=[
            pl.BlockSpec((tm, d), lambda i: (i, 0)),
            _mod_spec(d, row_fn, 5), _const_spec((1, d)),
            row_spec, row_spec,
            pl.BlockSpec((tm, GROUP_W), lambda i: (i, 1)),
            row_spec, row_spec, row_spec,
            _const_spec(w_out.shape),
        ],
        out_specs=pl.BlockSpec((tm, d), lambda i: (i, 0)),
        out_shape=jax.ShapeDtypeStruct((n_rows, d), F32),
        compiler_params=_cparams(("arbitrary",)),
        name="mixer_out",
    )(xs, mod3, g, hf, hb, lru_cols, yb, yc, yd, w_out)


def _pack_w_in(w_in):
    depth, d, _ = w_in.shape
    o_mla = COLS_LRU
    o_diff = o_mla + COLS_MLA
    z = lambda n: jnp.zeros((depth, d, n), w_in.dtype)
    lat = MLA_Q_RANK + MLA_KV_RANK
    parts = [
        w_in[:, :, :COLS_LRU],
        w_in[:, :, o_mla:o_mla + lat], z(MLA_NOPE),
        w_in[:, :, o_mla + lat:o_diff], z(MLA_PAD - MLA_NOPE - MLA_ROPE),
        w_in[:, :, o_diff:],
    ]
    return jnp.concatenate(parts, axis=-1).astype(BF16)


def _pack_mla(w_qb, w_kvb):
    depth = w_qb.shape[0]
    hd = MLA_NOPE + MLA_ROPE
    wq = w_qb.reshape(depth, MLA_Q_RANK, MLA_HEADS, hd)
    wq = jnp.pad(wq, ((0, 0), (0, 0), (0, 0), (0, MLA_PAD - hd)))
    wkv = w_kvb.reshape(depth, MLA_KV_RANK, MLA_HEADS, MLA_NOPE + MLA_V)
    wk = jnp.pad(wkv[..., :MLA_NOPE], ((0, 0), (0, 0), (0, 0), (0, MLA_PAD - MLA_NOPE)))
    wv = wkv[..., MLA_NOPE:]
    return (wq.reshape(depth, MLA_Q_RANK, MLA_HEADS * MLA_PAD).astype(BF16),
            wk.reshape(depth, MLA_KV_RANK, MLA_HEADS * MLA_PAD).astype(BF16),
            wv.reshape(depth, MLA_KV_RANK, MLA_HEADS * MLA_V).astype(BF16))


def _pack_gates(w_gates):
    depth = w_gates.shape[0]
    eye = jnp.eye(LRU_HEADS, dtype=w_gates.dtype)
    dense = jnp.einsum('ldghij,hk->ldghikj', w_gates, eye).reshape(depth, 2, 2, GROUP_W, GROUP_W)
    return jnp.concatenate([dense[:, :, 0], dense[:, :, 1]], axis=-1).astype(BF16)


def _rope_tables(rows, rot_dim, lane0, period, pad_rows):
    row = jnp.repeat(jnp.arange(rows, dtype=F32), GRID_W)
    col = jnp.tile(jnp.arange(GRID_W, dtype=F32), rows)
    per_axis = rot_dim // 2
    inv = ROPE_BASE ** (-jnp.arange(0, per_axis, 2, dtype=F32) / per_axis)
    ang = jnp.concatenate([row[:, None] * inv, col[:, None] * inv], axis=-1)
    cos, sin = jnp.cos(ang), jnp.sin(ang)
    lane = jnp.arange(LANES)
    rel = (lane - lane0) % period
    active = (lane >= lane0) & (rel < rot_dim)
    idx = rel % per_axis
    sign = jnp.where(rel < per_axis, -1.0, 1.0)
    cos_t = jnp.where(active[None, :], cos[:, idx], 1.0)
    sin_t = jnp.where(active[None, :], sin[:, idx] * sign[None, :], 0.0)
    cos_t = jnp.concatenate([cos_t, jnp.ones((pad_rows, LANES), F32)], axis=0)
    sin_t = jnp.concatenate([sin_t, jnp.zeros((pad_rows, LANES), F32)], axis=0)
    return cos_t, sin_t


def _pick(n, prefs):
    for p in prefs:
        if n % p == 0:
            return p
    return n


def kernel(x, c, ctx, c_ctx, w_mod, b_mod, norm_g, ffn_w_in, ffn_w_out, w_in, w_out,
           lru_conv_w, lru_conv_b, lru_w_gates, lru_b_gates, lru_lambda,
           mla_q_norm, mla_w_qb, mla_kv_norm, mla_w_kvb, diff_lambda, diff_norm, gqa_sink):
    batch, seq, d = x.shape
    ctx_len = ctx.shape[1]
    depth = w_mod.shape[0]
    d_ff = ffn_w_out.shape[2]
    n_lat, n_ctx = batch * seq, batch * ctx_len
    n_all = n_lat + n_ctx

    tm_ffn = _pick(math.gcd(seq, n_ctx), (1024, 512, 256))
    tf = _pick(d_ff, (256, 128))
    tm_mix = _pick(math.gcd(seq, ctx_len), (256, 128))
    tq = _pick(math.gcd(seq, ctx_len), (256, 128))
    tc = _pick(math.gcd(seq, ctx_len), (256, 128))
    tn_mod = _pick(N_MOD * d, (1024, 512, 256))
    assert seq % GRID_W == 0 and seq % WINDOW == 0 and seq >= 3 * WINDOW

    def rows_fn(tm):
        per, nlb = seq // tm, n_lat // tm
        return lambda i: jnp.where(i < nlb, i // per, batch)

    def tab_fn(tm):
        per, nlb = seq // tm, n_lat // tm
        return lambda i: jnp.where(i < nlb, i % per, per)

    mod_rows = SUBLANES * ((batch + 1 + SUBLANES - 1) // SUBLANES)
    cvec = jnp.concatenate([c, c_ctx[None, :], jnp.zeros((mod_rows - batch - 1, d), F32)], axis=0)
    mod = _modulation(cvec, w_mod, b_mod, tn_mod)

    w_in_p = _pack_w_in(w_in)
    wq_p, wk_p, wv_p = _pack_mla(mla_w_qb, mla_w_kvb)
    wg_p = _pack_gates(lru_w_gates)
    w_out_b = w_out.astype(BF16)
    rows = seq // GRID_W
    tabs = (_rope_tables(rows, HEAD_DIM, 0, HEAD_DIM, tm_mix)
            + _rope_tables(rows, MLA_ROPE, MLA_NOPE, LANES, tm_mix))

    xs = jnp.concatenate([x.reshape(n_lat, d), ctx.reshape(n_ctx, d)], axis=0)

    for layer in range(depth):
        last = layer == depth - 1
        lam_init = 0.8 - 0.6 * math.exp(-0.3 * layer)
        mod3 = mod[layer].reshape(mod_rows, 1, N_MOD * d)
        g = norm_g[layer][:, None, :]

        xs = _ffn_sublayer(xs, n_all, mod3, 0, g[0], g[1], ffn_w_in, ffn_w_out, layer, 0,
                           rows_fn(tm_ffn), tm_ffn, tf)

        (lru_cols, mq, mk, mv, dq, dk, dv, gq, gk, gv) = _mixer_in(
            xs, mod3, g[2], w_in_p[layer], tabs,
            (mla_q_norm[layer][None, :], wq_p[layer], mla_kv_norm[layer][None, :],
             wk_p[layer], wv_p[layer]),
            rows_fn(tm_mix), tab_fn(tm_mix), tm_mix)

        bg = lru_b_gates[layer].reshape(2, 1, 2 * GROUP_W)
        hf, hb = _lru_mixer(lru_cols, lru_conv_w[layer], lru_conv_b[layer][None, :],
                            wg_p[layer], bg, lru_lambda[layer][:, None, :],
                            batch, seq, ctx_len, tc)

        diff_extra = [diff_lambda[layer], diff_norm[layer][None, :]]
        diff_k = functools.partial(_diff_kernel, lam_init=lam_init)
        yb = _full_attention(_mla_kernel, mq, mk, mv, [], batch, seq, ctx_len, tq, True, GROUP_W)
        yc = _full_attention(diff_k, dq, dk, dv, diff_extra, batch, seq, ctx_len, tq, True, GROUP_W)
        yd = _gqa_attention(gq, gk, gv, gqa_sink[layer], batch, seq, ctx_len, True)
        if not last:
            yb_c = _full_attention(_mla_kernel, mq, mk, mv, [], batch, seq, ctx_len, tq, False, GROUP_W)
            yc_c = _full_attention(diff_k, dq, dk, dv, diff_extra, batch, seq, ctx_len, tq, False, GROUP_W)
            yd_c = _gqa_attention(gq, gk, gv, gqa_sink[layer], batch, seq, ctx_len, False)
            yb = jnp.concatenate([yb[:n_lat], yb_c[n_lat:]], axis=0)
            yc = jnp.concatenate([yc[:n_lat], yc_c[n_lat:]], axis=0)
            yd = jnp.concatenate([yd[:n_lat], yd_c[n_lat:]], axis=0)

        n_out = n_lat if last else n_all
        xs = _mixer_out(xs, n_out, mod3, g[3], hf, hb, lru_cols, yb, yc, yd, w_out_b[layer],
                        rows_fn(tm_mix), tm_mix)
        xs = _ffn_sublayer(xs, n_out, mod3, 6, g[4], g[5], ffn_w_in, ffn_w_out, layer, 1,
                           rows_fn(tm_ffn), tm_ffn, tf)

    return xs[:n_lat].reshape(batch, seq, d)
```

```python
import functools
import math

import jax
import jax.numpy as jnp
from jax import lax
from jax.experimental import pallas as pl
from jax.experimental.pallas import tpu as pltpu

F32 = jnp.float32
BF16 = jnp.bfloat16

GRID_W = 64
HEAD_DIM = 64
N_MOD = 9
GROUP_W = 512
LRU_HEADS = 8
LRU_BLOCK = 64
LRU_C = 8.0
MLA_HEADS = 8
MLA_NOPE = 64
MLA_ROPE = 32
MLA_V = 64
MLA_Q_RANK = 384
MLA_KV_RANK = 256
DIFF_HEADS = 4
DIFF_D = 64
DIFF_V = 128
GQA_Q_HEADS = 8
GQA_KV_HEADS = 2
GQA_GROUP = 4
WINDOW = 128
MACARON_WEIGHT = 0.5
ROPE_BASE = 10000.0
RMS_EPS = 1e-6
NEG_INF = -1e30
LOG2E = 1.4426950408889634

LANES = 128
SUBLANES = 8
MLA_PAD = 128

COLS_LRU = 2 * GROUP_W
COLS_MLA = MLA_Q_RANK + MLA_KV_RANK + MLA_ROPE
DIFF_QK = 2 * DIFF_HEADS * 2 * DIFF_D
COLS_DIFF = DIFF_QK + DIFF_HEADS * DIFF_V
GQA_QK = (GQA_Q_HEADS + GQA_KV_HEADS) * HEAD_DIM
GQA_KVW = GQA_KV_HEADS * HEAD_DIM
COLS_GQA = GQA_QK + GQA_KVW
P_LRU = 0
P_MLA = COLS_LRU
W_MLA = 768
P_DIFF = P_MLA + W_MLA
P_GQA = P_DIFF + DIFF_QK
N_IN_P = P_GQA + GQA_QK

VMEM_LIMIT = 60 * 1024 * 1024


def _cparams(sem):
    return pltpu.CompilerParams(dimension_semantics=sem, vmem_limit_bytes=VMEM_LIMIT)


def _rms(x, g):
    ms = jnp.mean(x * x, axis=-1, keepdims=True)
    return x * lax.rsqrt(ms + RMS_EPS) * g


def _dot(a, b):
    return jnp.dot(a, b, preferred_element_type=F32)


def _dot_nt(a, b):
    return lax.dot_general(a, b, (((1,), (1,)), ((), ())), preferred_element_type=F32)


def _mod_kernel(c_ref, w_ref, b_ref, o_ref):
    s = c_ref[...]
    s = s * jax.nn.sigmoid(s)
    o_ref[...] = _dot(s.astype(BF16), w_ref[...].astype(BF16)) + b_ref[...]


def _modulation(cvec, w_mod, b_mod, tn):
    depth, d, n = w_mod.shape
    rows = cvec.shape[0]
    return pl.pallas_call(
        _mod_kernel,
        grid=(depth, n // tn),
        in_specs=[
            pl.BlockSpec((rows, d), lambda l, j: (0, 0)),
            pl.BlockSpec((None, d, tn), lambda l, j: (l, 0, j)),
            pl.BlockSpec((None, 1, tn), lambda l, j: (l, 0, j)),
        ],
        out_specs=pl.BlockSpec((None, rows, tn), lambda l, j: (l, 0, j)),
        out_shape=jax.ShapeDtypeStruct((depth, rows, n), F32),
        compiler_params=_cparams(("arbitrary", "arbitrary")),
        name="modulation",
    )(cvec, w_mod, b_mod.reshape(depth, 1, n))


FFN_ROW_CHUNK = 64


def _ffn_kernel(x_ref, shift_ref, scale_ref, gate_ref, gpre_ref, gpost_ref,
                wg_ref, wu_ref, wo_ref, o_ref, h_ref, *, nj):
    j = pl.program_id(1)
    n_chunks = x_ref.shape[0] // FFN_ROW_CHUNK

    def rows(r):
        return pl.ds(pl.multiple_of(r * FFN_ROW_CHUNK, FFN_ROW_CHUNK), FFN_ROW_CHUNK)

    @pl.when(j == 0)
    def _():
        mul = gpre_ref[...] * (1.0 + scale_ref[...])
        shift = shift_ref[...]

        def body(r, carry):
            x = x_ref[rows(r), :]
            ms = jnp.mean(x * x, axis=-1, keepdims=True)
            h_ref[rows(r), :] = (x * lax.rsqrt(ms + RMS_EPS) * mul + shift).astype(BF16)
            return carry

        lax.fori_loop(0, n_chunks, body, 0, unroll=2)
        o_ref[...] = jnp.zeros_like(o_ref)

    h = h_ref[...]
    g = _dot(h, wg_ref[...].astype(BF16))
    u = _dot(h, wu_ref[...].astype(BF16))
    a = (g * jax.nn.sigmoid(g) * u).astype(BF16)
    o_ref[...] += _dot(a, wo_ref[...].astype(BF16))

    @pl.when(j == nj - 1)
    def _():
        mul = (MACARON_WEIGHT * gate_ref[...]) * gpost_ref[...]

        def body(r, carry):
            y = o_ref[rows(r), :]
            ms = jnp.mean(y * y, axis=-1, keepdims=True)
            o_ref[rows(r), :] = x_ref[rows(r), :] + y * lax.rsqrt(ms + RMS_EPS) * mul
            return carry

        lax.fori_loop(0, n_chunks, body, 0, unroll=2)


def _mod_spec(d, row_fn, k):
    return pl.BlockSpec((None, 1, d), lambda i, *_: (row_fn(i), 0, k))


def _ffn_sublayer(xs, n_rows, mod3, k0, g_pre, g_post, w_in_all, w_out_all, layer, sub,
                  row_fn, tm, tf):
    d = xs.shape[1]
    d_ff = w_out_all.shape[2]
    nj = d_ff // tf
    return pl.pallas_call(
        functools.partial(_ffn_kernel, nj=nj),
        grid=(n_rows // tm, nj),
        in_specs=[
            pl.BlockSpec((tm, d), lambda i, j: (i, 0), pipeline_mode=pl.Buffered(1)),
            _mod_spec(d, row_fn, k0), _mod_spec(d, row_fn, k0 + 1), _mod_spec(d, row_fn, k0 + 2),
            pl.BlockSpec((1, d), lambda i, j: (0, 0)),
            pl.BlockSpec((1, d), lambda i, j: (0, 0)),
            pl.BlockSpec((None, None, d, tf), lambda i, j: (layer, sub, 0, j)),
            pl.BlockSpec((None, None, d, tf), lambda i, j: (layer, sub, 0, nj + j)),
            pl.BlockSpec((None, None, tf, d), lambda i, j: (layer, sub, j, 0)),
        ],
        out_specs=pl.BlockSpec((tm, d), lambda i, j: (i, 0)),
        out_shape=jax.ShapeDtypeStruct((n_rows, d), F32),
        scratch_shapes=[pltpu.VMEM((tm, d), BF16)],
        compiler_params=_cparams(("arbitrary", "arbitrary")),
        name="ffn_sublayer",
    )(xs, mod3, mod3, mod3, g_pre, g_post, w_in_all, w_in_all, w_out_all)


def _swap_halves(x, first, half):
    return jnp.where(first, pltpu.roll(x, LANES - half, 1), pltpu.roll(x, half, 1))


def _rope_chunks(x, cos, sin, first, half):
    out = []
    for c in range(x.shape[1] // LANES):
        blk = x[:, c * LANES:(c + 1) * LANES]
        out.append(blk * cos + _swap_halves(blk, first, half) * sin)
    return out[0] if len(out) == 1 else jnp.concatenate(out, axis=1)


def _mixer_in_kernel(x_ref, shift_ref, scale_ref, g_ref, w_ref, wvt_ref,
                     cos_hd_ref, sin_hd_ref, cos_mla_ref, sin_mla_ref,
                     qn_ref, wq_ref, kvn_ref, wk_ref, wmvt_ref,
                     lru_ref, mq_ref, mk_ref, mvt_ref, dq_ref, dk_ref, dvt_ref,
                     gq_ref, gk_ref, gvt_ref):
    h = (_rms(x_ref[...], g_ref[...]) * (1.0 + scale_ref[...]) + shift_ref[...]).astype(BF16)
    tm = h.shape[0]
    lane = lax.broadcasted_iota(jnp.int32, (tm, LANES), 1)
    first_hd = (lane % HEAD_DIM) < (HEAD_DIM // 2)
    first_mla = lane < (MLA_NOPE + MLA_ROPE // 2)
    cos_hd, sin_hd = cos_hd_ref[...], sin_hd_ref[...]
    cos_mla, sin_mla = cos_mla_ref[...], sin_mla_ref[...]

    lru_ref[...] = _dot(h, w_ref[:, P_LRU:P_LRU + COLS_LRU])

    cm = _dot(h, w_ref[:, P_MLA:P_MLA + W_MLA])
    qn = _rms(cm[:, :MLA_Q_RANK], qn_ref[...]).astype(BF16)
    q = _rope_chunks(_dot(qn, wq_ref[...]), cos_mla, sin_mla, first_mla, MLA_ROPE // 2)
    mq_ref[...] = (q * (LOG2E * (MLA_NOPE + MLA_ROPE) ** -0.5)).astype(BF16)
    kvn = _rms(cm[:, MLA_Q_RANK:MLA_Q_RANK + MLA_KV_RANK], kvn_ref[...]).astype(BF16)
    k_rope = _rope_chunks(cm[:, W_MLA - LANES:], cos_mla, sin_mla, first_mla, MLA_ROPE // 2)
    k_nope = _dot(kvn, wk_ref[...])
    mk_ref[...] = (k_nope + jnp.concatenate([k_rope] * MLA_HEADS, axis=1)).astype(BF16)
    mvt_ref[...] = _dot_nt(wmvt_ref[...], kvn).astype(BF16)

    cd = _dot(h, w_ref[:, P_DIFF:P_DIFF + DIFF_QK])
    half = DIFF_QK // 2
    dq_ref[...] = (_rope_chunks(cd[:, :half], cos_hd, sin_hd, first_hd, HEAD_DIM // 2)
                   * (LOG2E * DIFF_D ** -0.5)).astype(BF16)
    dk_ref[...] = _rope_chunks(cd[:, half:], cos_hd, sin_hd, first_hd, HEAD_DIM // 2).astype(BF16)

    nq = GQA_Q_HEADS * HEAD_DIM
    cg = _dot(h, w_ref[:, P_GQA:P_GQA + GQA_QK])
    gq_ref[...] = (_rope_chunks(cg[:, :nq], cos_hd, sin_hd, first_hd, HEAD_DIM // 2)
                   * (LOG2E * HEAD_DIM ** -0.5)).astype(BF16)
    gk_ref[...] = _rope_chunks(cg[:, nq:], cos_hd, sin_hd, first_hd, HEAD_DIM // 2).astype(BF16)

    vt = _dot_nt(wvt_ref[...], h)
    dvt_ref[...] = vt[:GROUP_W].astype(BF16)
    gvt_ref[...] = vt[GROUP_W:].astype(BF16)


def _const_spec(shape):
    nd = len(shape)
    return pl.BlockSpec(shape, lambda *_: (0,) * nd, pipeline_mode=pl.Buffered(1))


def _mixer_in(xs, mod3, g, w_in_p, w_vt, tabs, mla_w, row_fn, tab_fn, tm):
    n_rows, d = xs.shape
    cos_hd, sin_hd, cos_mla, sin_mla = tabs
    qn, wq, kvn, wk, wmvt = mla_w
    tab_spec = pl.BlockSpec((tm, LANES), lambda i: (tab_fn(i), 0))
    outs = [(COLS_LRU, F32, False),
            (MLA_HEADS * MLA_PAD, BF16, False), (MLA_HEADS * MLA_PAD, BF16, False), (GROUP_W, BF16, True),
            (GROUP_W, BF16, False), (GROUP_W, BF16, False), (GROUP_W, BF16, True),
            (GROUP_W, BF16, False), (GQA_KVW, BF16, False), (GQA_KVW, BF16, True)]
    out_specs = [pl.BlockSpec((w, tm), lambda i: (0, i)) if t else pl.BlockSpec((tm, w), lambda i: (i, 0))
                 for w, _, t in outs]
    out_shape = [jax.ShapeDtypeStruct((w, n_rows) if t else (n_rows, w), dt) for w, dt, t in outs]
    return pl.pallas_call(
        _mixer_in_kernel,
        grid=(n_rows // tm,),
        in_specs=[
            pl.BlockSpec((tm, d), lambda i: (i, 0)),
            _mod_spec(d, row_fn, 3), _mod_spec(d, row_fn, 4),
            _const_spec((1, d)), _const_spec(w_in_p.shape), _const_spec(w_vt.shape),
            tab_spec, tab_spec, tab_spec, tab_spec,
            _const_spec(qn.shape), _const_spec(wq.shape), _const_spec(kvn.shape),
            _const_spec(wk.shape), _const_spec(wmvt.shape),
        ],
        out_specs=out_specs,
        out_shape=out_shape,
        compiler_params=_cparams(("arbitrary",)),
        name="mixer_in",
    )(xs, mod3, mod3, g, w_in_p, w_vt, cos_hd, sin_hd, cos_mla, sin_mla, qn, wq, kvn, wk, wmvt)


def _sigmoid(x):
    return 0.5 * jnp.tanh(0.5 * x) + 0.5


def _lru_direction(x, prev, nxt, cw, cb, wg, bg, lam, reverse, carry_ref, h_ref, a_scr, b_scr):
    tc, ch = x.shape
    ext = jnp.concatenate([prev, x, nxt], axis=0)
    n_ext = tc + 2 * SUBLANES

    def tap(shift):
        return pltpu.roll(ext, (-shift) % n_ext, 0)[SUBLANES:SUBLANES + tc]

    u = cw[0:1] * tap(-2) + cw[1:2] * tap(-1) + cw[2:3] * x + cw[3:4] * tap(1) + cb
    g = _dot(u.astype(BF16), wg) + bg
    r = _sigmoid(g[:, :ch])
    i = _sigmoid(g[:, ch:])
    log_a = -LRU_C * r * jax.nn.softplus(-lam)
    a = jnp.exp(log_a)
    b = jnp.sqrt((1.0 - a) * (1.0 + a)) * (i * u)

    nt = tc // SUBLANES
    a = a.reshape(nt, SUBLANES, ch)
    b = b.reshape(nt, SUBLANES, ch)
    row = lax.broadcasted_iota(jnp.int32, (nt, SUBLANES, ch), 1)
    for s in (1, 2, 4):
        ok = (row < SUBLANES - s) if reverse else (row >= s)
        shift = SUBLANES - s if reverse else s
        a_sh, b_sh = pltpu.roll(a, shift, 1), pltpu.roll(b, shift, 1)
        b = a * jnp.where(ok, b_sh, 0.0) + b
        a = a * jnp.where(ok, a_sh, 1.0)
    a_scr[...] = a.reshape(tc, ch)
    b_scr[...] = b.reshape(tc, ch)


    def body(k, carry):
        kk = nt - 1 - k if reverse else k
        sl = pl.ds(pl.multiple_of(kk * SUBLANES, SUBLANES), SUBLANES)
        ht = b_scr[sl, :] + a_scr[sl, :] * carry
        h_ref[sl, :] = ht
        edge = ht[0:1, :] if reverse else ht[SUBLANES - 1:SUBLANES, :]
        return jnp.broadcast_to(edge, (SUBLANES, ch))

    carry_ref[...] = lax.fori_loop(0, nt, body, carry_ref[...])


def _lru_kernel(xf_ref, pf_ref, nf_ref, xb_ref, pb_ref, nb_ref,
                cw_ref, cb_ref, wgf_ref, wgb_ref, bgf_ref, bgb_ref, lamf_ref, lamb_ref,
                hf_ref, hb_ref, cf_ref, cbk_ref, a_scr, b_scr, *, halo_fn):
    s = pl.program_id(1)

    @pl.when(s == 0)
    def _():
        cf_ref[...] = jnp.zeros_like(cf_ref)
        cbk_ref[...] = jnp.zeros_like(cbk_ref)

    pvf, nvf, pvb, nvb = halo_fn(s)
    cw, cb = cw_ref[...], cb_ref[...]
    _lru_direction(xf_ref[...], pf_ref[...] * pvf, nf_ref[...] * nvf, cw, cb,
                   wgf_ref[...], bgf_ref[...], lamf_ref[...], False, cf_ref, hf_ref, a_scr, b_scr)
    _lru_direction(xb_ref[...], pb_ref[...] * pvb, nb_ref[...] * nvb, cw, cb,
                   wgb_ref[...], bgb_ref[...], lamb_ref[...], True, cbk_ref, hb_ref, a_scr, b_scr)


def _lru_mixer(lru_cols, conv_w, conv_b, wg, bg, lam, batch, seq, ctx_len, tc):
    n_rows = lru_cols.shape[0]
    ch = GROUP_W
    nl, ncx = seq // tc, ctx_len // tc
    n_steps = nl + ncx
    r8 = tc // SUBLANES
    last8 = n_rows // SUBLANES - 1

    def chunk_f(b, s):
        return jnp.where(s < ncx, batch * nl + b * ncx + s, b * nl + s - ncx)

    def chunk_b(b, s):
        return jnp.where(s < ncx, batch * nl + b * ncx + ncx - 1 - s, b * nl + nl - 1 - (s - ncx))

    def seg_pos(s, reverse):
        in_ctx = s < ncx
        if reverse:
            idx = jnp.where(in_ctx, ncx - 1 - s, nl - 1 - (s - ncx))
        else:
            idx = jnp.where(in_ctx, s, s - ncx)
        return idx, jnp.where(in_ctx, ncx, nl)

    def halo_fn(s):
        out = []
        for reverse in (False, True):
            idx, n = seg_pos(s, reverse)
            out += [(idx > 0).astype(F32), (idx < n - 1).astype(F32)]
        return out

    def x_spec(fn):
        return pl.BlockSpec((tc, ch), lambda b, s: (fn(b, s), 0))

    def prev_spec(fn):
        return pl.BlockSpec((SUBLANES, ch), lambda b, s: (jnp.maximum(fn(b, s) * r8 - 1, 0), 0))

    def next_spec(fn):
        return pl.BlockSpec((SUBLANES, ch), lambda b, s: (jnp.minimum((fn(b, s) + 1) * r8, last8), 0))

    def c_spec(shape):
        return pl.BlockSpec(shape, lambda b, s: (0,) * len(shape))

    return pl.pallas_call(
        functools.partial(_lru_kernel, halo_fn=halo_fn),
        grid=(batch, n_steps),
        in_specs=[
            x_spec(chunk_f), prev_spec(chunk_f), next_spec(chunk_f),
            x_spec(chunk_b), prev_spec(chunk_b), next_spec(chunk_b),
            c_spec(conv_w.shape), c_spec(conv_b.shape),
            c_spec(wg[0].shape), c_spec(wg[1].shape), c_spec(bg[0].shape), c_spec(bg[1].shape),
            c_spec(lam[0].shape), c_spec(lam[1].shape),
        ],
        out_specs=[x_spec(chunk_f), x_spec(chunk_b)],
        out_shape=[jax.ShapeDtypeStruct((n_rows, ch), F32)] * 2,
        scratch_shapes=[pltpu.VMEM((SUBLANES, ch), F32), pltpu.VMEM((SUBLANES, ch), F32),
                        pltpu.VMEM((tc, ch), F32), pltpu.VMEM((tc, ch), F32)],
        compiler_params=_cparams(("arbitrary", "arbitrary")),
        name="rglru",
    )(lru_cols, lru_cols, lru_cols, lru_cols, lru_cols, lru_cols,
      conv_w, conv_b, wg[0], wg[1], bg[0], bg[1], lam[0], lam[1])


def _softmax_cols(scores, extra=None):
    m = scores[0].max(axis=0, keepdims=True)
    for s in scores[1:]:
        m = jnp.maximum(m, s.max(axis=0, keepdims=True))
    if extra is not None:
        m = jnp.maximum(m, extra)
    ps = [jnp.exp2(s - m) for s in scores]
    den = ps[0].sum(axis=0, keepdims=True)
    for p in ps[1:]:
        den = den + p.sum(axis=0, keepdims=True)
    if extra is not None:
        den = den + jnp.exp2(extra - m)
    return ps, den


def _two_block_queries(q2, width):
    lane = lax.broadcasted_iota(jnp.int32, q2.shape, 1)
    zero = jnp.zeros((), q2.dtype)
    return jnp.concatenate([jnp.where(lane < width, q2, zero), jnp.where(lane >= width, q2, zero)], axis=0)


def _paired_scores(q_ref, ks, width, n_iter, finish, s_bufs):
    def issue(i):
        sl = slice(2 * i * width, 2 * (i + 1) * width)
        qq = _two_block_queries(q_ref[:, sl], width)
        buf, row = s_bufs[i % 2], 0
        for k in ks:
            buf[row:row + k.shape[0], :] = _dot_nt(k[:, sl], qq)
            row += k.shape[0]

    issue(0)
    for i in range(n_iter):
        if i + 1 < n_iter:
            issue(i + 1)
        finish(i, s_bufs[i % 2][...])


def _seg_rows(vts):
    rows, out = 0, []
    for vt in vts:
        out.append(slice(rows, rows + vt.shape[1]))
        rows += vt.shape[1]
    return out


def _mla_kernel(q_ref, *refs, n_seg):
    ks, vts = refs[:n_seg], refs[n_seg:2 * n_seg]
    o_ref, s_bufs = refs[2 * n_seg], refs[2 * n_seg + 1:]
    tq = q_ref.shape[0]
    segs = _seg_rows(vts)
    outs = []

    def finish(hp, s):
        ps, den = _softmax_cols([s])
        p = ps[0].astype(BF16)
        for e in range(2):
            h = 2 * hp + e
            cols = slice(e * tq, (e + 1) * tq)
            o = None
            for rows, vt in zip(segs, vts):
                t = _dot(vt[h * MLA_V:(h + 1) * MLA_V, :], p[rows, cols])
                o = t if o is None else o + t
            outs.append(o / den[:, cols])

    _paired_scores(q_ref, ks, MLA_PAD, MLA_HEADS // 2, finish, s_bufs)
    o_ref[...] = jnp.concatenate(outs, axis=0).T.astype(o_ref.dtype)


def _diff_kernel(q_ref, lv_ref, g_ref, *refs, n_seg, lam_init):
    ks, vts = refs[:n_seg], refs[n_seg:2 * n_seg]
    o_ref, s_bufs = refs[2 * n_seg], refs[2 * n_seg + 1:]
    lv = lv_ref[...]
    lam = (jnp.exp(jnp.sum(lv[0:1] * lv[1:2], axis=-1, keepdims=True))
           - jnp.exp(jnp.sum(lv[2:3] * lv[3:4], axis=-1, keepdims=True)) + lam_init)
    tq = q_ref.shape[0]
    segs = _seg_rows(vts)
    outs = []

    def finish(h, s):
        ps, den = _softmax_cols([s])
        c1 = 1.0 / den[:, :tq]
        c2 = lam / den[:, tq:]
        w = (ps[0][:, :tq] * c1 - ps[0][:, tq:] * c2).astype(BF16)
        o = None
        for rows, vt in zip(segs, vts):
            t = _dot(vt[h * DIFF_V:(h + 1) * DIFF_V, :], w[rows, :])
            o = t if o is None else o + t
        outs.append(o)

    _paired_scores(q_ref, ks, DIFF_D, DIFF_HEADS, finish, s_bufs)
    o = jnp.concatenate(outs, axis=0).T
    g = g_ref[...]
    for h in range(DIFF_HEADS):
        sl = slice(h * DIFF_V, (h + 1) * DIFF_V)
        o_ref[:, sl] = (_rms(o[:, sl], g) * (1.0 - lam_init)).astype(o_ref.dtype)


def _skip_first(kernel):
    def wrapped(_, *refs, **kw):
        return kernel(*refs, **kw)
    return wrapped


def _full_attention(kernel, q, k, vt, extra, batch, seq, ctx_len, tq, prev_out):
    n_rows = q.shape[0]
    qw, kw, vw = q.shape[1], k.shape[1], vt.shape[0]
    ctx_blk0 = batch * seq // ctx_len

    ctx_k = pl.BlockSpec((ctx_len, kw), lambda b, i: (ctx_blk0 + b, 0))
    ctx_v = pl.BlockSpec((vw, ctx_len), lambda b, i: (0, ctx_blk0 + b))
    lat_k = pl.BlockSpec((seq, kw), lambda b, i: (b, 0))
    lat_v = pl.BlockSpec((vw, seq), lambda b, i: (0, b))
    extra_specs = [pl.BlockSpec(e.shape, lambda b, i, nd=e.ndim: (0,) * nd) for e in extra]
    if prev_out is None:
        nq = seq // tq
        q_map = lambda b, i: (b * nq + i, 0)
        kv_specs, kv_args, n_seg = [lat_k, ctx_k, lat_v, ctx_v], [k, k, vt, vt], 2
        lead_specs, lead_args, aliases = [], [], {}
        body = functools.partial(kernel, n_seg=n_seg)
    else:
        nq = ctx_len // tq
        q_map = lambda b, i: (batch * seq // tq + b * nq + i, 0)
        kv_specs, kv_args, n_seg = [ctx_k, ctx_v], [k, vt], 1
        lead_specs, lead_args, aliases = [pl.BlockSpec(memory_space=pl.ANY)], [prev_out], {0: 0}
        body = functools.partial(_skip_first(kernel), n_seg=n_seg)
    n_keys = seq + ctx_len if prev_out is None else ctx_len
    return pl.pallas_call(
        body,
        grid=(batch, nq),
        in_specs=lead_specs + [pl.BlockSpec((tq, qw), q_map)] + extra_specs + kv_specs,
        out_specs=pl.BlockSpec((tq, GROUP_W), q_map),
        out_shape=jax.ShapeDtypeStruct((n_rows, GROUP_W), BF16),
        scratch_shapes=[pltpu.VMEM((n_keys, 2 * tq), F32)] * 2,
        input_output_aliases=aliases,
        compiler_params=_cparams(("arbitrary", "arbitrary")),
        name="full_attention",
    )(*lead_args, q, *extra, *kv_args)


def _gqa_group(q_ref, kv, ks, vts, masks, sink_ref, outs):
    ksl = slice(kv * HEAD_DIM, (kv + 1) * HEAD_DIM)
    heads = range(kv * GQA_GROUP, (kv + 1) * GQA_GROUP)
    nq = q_ref.shape[0]
    q4 = jnp.concatenate([q_ref[:, h * HEAD_DIM:(h + 1) * HEAD_DIM] for h in heads], axis=0)
    scores = []
    for k, mask in zip(ks, masks):
        s = _dot_nt(k[:, ksl], q4)
        scores.append(s if mask is None else jnp.where(mask, s, NEG_INF))
    sink = jnp.concatenate([jnp.full((1, nq), sink_ref[h] * LOG2E, F32) for h in heads], axis=1)
    ps, den = _softmax_cols(scores, extra=sink)
    o = None
    for p, vt in zip(ps, vts):
        t = _dot(vt[ksl, :], p.astype(BF16))
        o = t if o is None else o + t
    o = o / den
    outs += [o[:, g * nq:(g + 1) * nq] for g in range(GQA_GROUP)]


def _gqa_lat_kernel(sink_ref, q_ref, kl_ref, vtl_ref, kc_ref, vtc_ref, o_ref, *, seq):
    n = pl.program_id(1)
    span = 3 * WINDOW
    start = pl.multiple_of(jnp.clip((n - 1) * WINDOW, 0, seq - span), WINDOW)
    kw = kl_ref[pl.ds(start, span), :]
    vtw = vtl_ref[:, pl.ds(start, span)]
    kpos = start + lax.broadcasted_iota(jnp.int32, (span, GQA_GROUP * WINDOW), 0)
    qpos = n * WINDOW + lax.broadcasted_iota(jnp.int32, (span, GQA_GROUP * WINDOW), 1) % WINDOW
    valid = jnp.abs(kpos - qpos) <= WINDOW
    outs = []
    for kv in range(GQA_KV_HEADS):
        _gqa_group(q_ref, kv, [kc_ref[...], kw], [vtc_ref[...], vtw], [None, valid], sink_ref, outs)
    o_ref[...] = jnp.concatenate(outs, axis=0).T.astype(o_ref.dtype)


def _gqa_ctx_kernel(_, sink_ref, q_ref, kc_ref, vtc_ref, o_ref):
    outs = []
    for kv in range(GQA_KV_HEADS):
        _gqa_group(q_ref, kv, [kc_ref[...]], [vtc_ref[...]], [None], sink_ref, outs)
    o_ref[...] = jnp.concatenate(outs, axis=0).T.astype(o_ref.dtype)


def _gqa_attention(q, k, vt, sink, batch, seq, ctx_len, prev_out):
    n_rows = q.shape[0]
    ctx_blk0 = batch * seq // ctx_len
    smem = pl.BlockSpec(memory_space=pltpu.SMEM)
    ctx_k = pl.BlockSpec((ctx_len, GQA_KVW), lambda b, i: (ctx_blk0 + b, 0))
    ctx_v = pl.BlockSpec((GQA_KVW, ctx_len), lambda b, i: (0, ctx_blk0 + b))
    if prev_out is None:
        tq = WINDOW
        nq = seq // tq
        q_map = lambda b, i: (b * nq + i, 0)
        lat_k = pl.BlockSpec((seq, GQA_KVW), lambda b, i: (b, 0))
        lat_v = pl.BlockSpec((GQA_KVW, seq), lambda b, i: (0, b))
        kernel = functools.partial(_gqa_lat_kernel, seq=seq)
        specs, args = [smem, pl.BlockSpec((tq, GROUP_W), q_map), lat_k, lat_v, ctx_k, ctx_v], [sink, q, k, vt, k, vt]
        aliases = {}
    else:
        tq = WINDOW
        nq = ctx_len // tq
        q_map = lambda b, i: (batch * seq // tq + b * nq + i, 0)
        kernel = _gqa_ctx_kernel
        specs = [pl.BlockSpec(memory_space=pl.ANY), smem, pl.BlockSpec((tq, GROUP_W), q_map), ctx_k, ctx_v]
        args = [prev_out, sink, q, k, vt]
        aliases = {0: 0}
    return pl.pallas_call(
        kernel,
        grid=(batch, nq),
        in_specs=specs,
        out_specs=pl.BlockSpec((tq, GROUP_W), q_map),
        out_shape=jax.ShapeDtypeStruct((n_rows, GROUP_W), BF16),
        input_output_aliases=aliases,
        compiler_params=_cparams(("arbitrary", "arbitrary")),
        name="gqa_attention",
    )(*args)


def _mixer_out_kernel(x_ref, gate_ref, g_ref, hf_ref, hb_ref, lg_ref, yb_ref, yc_ref, yd_ref,
                      w_ref, o_ref):
    ya = ((hf_ref[...] + hb_ref[...]) * jax.nn.gelu(lg_ref[...])).astype(BF16)
    y = _dot(ya, w_ref[0:GROUP_W, :])
    y += _dot(yb_ref[...], w_ref[GROUP_W:2 * GROUP_W, :])
    y += _dot(yc_ref[...], w_ref[2 * GROUP_W:3 * GROUP_W, :])
    y += _dot(yd_ref[...], w_ref[3 * GROUP_W:4 * GROUP_W, :])
    o_ref[...] = x_ref[...] + gate_ref[...] * _rms(y, g_ref[...])


def _mixer_out(xs, n_rows, mod3, g, hf, hb, lru_cols, yb, yc, yd, w_out, row_fn, tm):
    d = xs.shape[1]
    row_spec = pl.BlockSpec((tm, GROUP_W), lambda i: (i, 0))
    return pl.pallas_call(
        _mixer_out_kernel,
        grid=(n_rows // tm,),
        in_specs=[
            pl.BlockSpec((tm, d), lambda i: (i, 0)),
            _mod_spec(d, row_fn, 5), _const_spec((1, d)),
            row_spec, row_spec,
            pl.BlockSpec((tm, GROUP_W), lambda i: (i, 1)),
            row_spec, row_spec, row_spec,
            _const_spec(w_out.shape),
        ],
        out_specs=pl.BlockSpec((tm, d), lambda i: (i, 0)),
        out_shape=jax.ShapeDtypeStruct((n_rows, d), F32),
        compiler_params=_cparams(("arbitrary",)),
        name="mixer_out",
    )(xs, mod3, g, hf, hb, lru_cols, yb, yc, yd, w_out)


def _pack_w_in(w_in):
    depth, d, _ = w_in.shape
    o_mla = COLS_LRU
    o_diff = o_mla + COLS_MLA
    o_gqa = o_diff + COLS_DIFF
    z = lambda n: jnp.zeros((depth, d, n), w_in.dtype)
    lat = MLA_Q_RANK + MLA_KV_RANK
    parts = [
        w_in[:, :, :COLS_LRU],
        w_in[:, :, o_mla:o_mla + lat], z(MLA_NOPE),
        w_in[:, :, o_mla + lat:o_diff], z(MLA_PAD - MLA_NOPE - MLA_ROPE),
        w_in[:, :, o_diff:o_diff + DIFF_QK],
        w_in[:, :, o_gqa:o_gqa + GQA_QK],
    ]
    vals = jnp.concatenate([w_in[:, :, o_diff + DIFF_QK:o_gqa], w_in[:, :, o_gqa + GQA_QK:]], axis=-1)
    return jnp.concatenate(parts, axis=-1).astype(BF16), jnp.swapaxes(vals, 1, 2).astype(BF16)


def _pack_mla(w_qb, w_kvb):
    depth = w_qb.shape[0]
    hd = MLA_NOPE + MLA_ROPE
    wq = w_qb.reshape(depth, MLA_Q_RANK, MLA_HEADS, hd)
    wq = jnp.pad(wq, ((0, 0), (0, 0), (0, 0), (0, MLA_PAD - hd)))
    wkv = w_kvb.reshape(depth, MLA_KV_RANK, MLA_HEADS, MLA_NOPE + MLA_V)
    wk = jnp.pad(wkv[..., :MLA_NOPE], ((0, 0), (0, 0), (0, 0), (0, MLA_PAD - MLA_NOPE)))
    wv = wkv[..., MLA_NOPE:].reshape(depth, MLA_KV_RANK, MLA_HEADS * MLA_V)
    return (wq.reshape(depth, MLA_Q_RANK, MLA_HEADS * MLA_PAD).astype(BF16),
            wk.reshape(depth, MLA_KV_RANK, MLA_HEADS * MLA_PAD).astype(BF16),
            jnp.swapaxes(wv, 1, 2).astype(BF16))


def _pack_gates(w_gates):
    depth = w_gates.shape[0]
    eye = jnp.eye(LRU_HEADS, dtype=w_gates.dtype)
    dense = jnp.einsum('ldghij,hk->ldghikj', w_gates, eye).reshape(depth, 2, 2, GROUP_W, GROUP_W)
    return jnp.concatenate([dense[:, :, 0], dense[:, :, 1]], axis=-1).astype(BF16)


def _rope_tables(rows, rot_dim, lane0, period, pad_rows):
    row = jnp.repeat(jnp.arange(rows, dtype=F32), GRID_W)
    col = jnp.tile(jnp.arange(GRID_W, dtype=F32), rows)
    per_axis = rot_dim // 2
    inv = ROPE_BASE ** (-jnp.arange(0, per_axis, 2, dtype=F32) / per_axis)
    ang = jnp.concatenate([row[:, None] * inv, col[:, None] * inv], axis=-1)
    cos, sin = jnp.cos(ang), jnp.sin(ang)
    lane = jnp.arange(LANES)
    rel = (lane - lane0) % period
    active = (lane >= lane0) & (rel < rot_dim)
    idx = rel % per_axis
    sign = jnp.where(rel < per_axis, -1.0, 1.0)
    cos_t = jnp.where(active[None, :], cos[:, idx], 1.0)
    sin_t = jnp.where(active[None, :], sin[:, idx] * sign[None, :], 0.0)
    cos_t = jnp.concatenate([cos_t, jnp.ones((pad_rows, LANES), F32)], axis=0)
    sin_t = jnp.concatenate([sin_t, jnp.zeros((pad_rows, LANES), F32)], axis=0)
    return cos_t, sin_t


def _pick(n, prefs):
    for p in prefs:
        if n % p == 0:
            return p
    return n


def kernel(x, c, ctx, c_ctx, w_mod, b_mod, norm_g, ffn_w_in, ffn_w_out, w_in, w_out,
           lru_conv_w, lru_conv_b, lru_w_gates, lru_b_gates, lru_lambda,
           mla_q_norm, mla_w_qb, mla_kv_norm, mla_w_kvb, diff_lambda, diff_norm, gqa_sink):
    batch, seq, d = x.shape
    ctx_len = ctx.shape[1]
    depth = w_mod.shape[0]
    d_ff = ffn_w_out.shape[2]
    n_lat, n_ctx = batch * seq, batch * ctx_len
    n_all = n_lat + n_ctx

    tm_ffn = _pick(math.gcd(seq, n_ctx), (1024, 512, 256))
    tf = _pick(d_ff, (256, 128))
    tm_mix = _pick(math.gcd(seq, ctx_len), (256, 128))
    tq = _pick(math.gcd(seq, ctx_len), (256, 128))
    tc = _pick(math.gcd(seq, ctx_len), (256, 128))
    tn_mod = _pick(N_MOD * d, (1024, 512, 256))
    assert seq % GRID_W == 0 and seq % WINDOW == 0 and seq >= 3 * WINDOW and ctx_len % WINDOW == 0

    def rows_fn(tm):
        per, nlb = seq // tm, n_lat // tm
        return lambda i: jnp.where(i < nlb, i // per, batch)

    def tab_fn(tm):
        per, nlb = seq // tm, n_lat // tm
        return lambda i: jnp.where(i < nlb, i % per, per)

    mod_rows = SUBLANES * ((batch + 1 + SUBLANES - 1) // SUBLANES)
    cvec = jnp.concatenate([c, c_ctx[None, :], jnp.zeros((mod_rows - batch - 1, d), F32)], axis=0)
    mod = _modulation(cvec, w_mod, b_mod, tn_mod)

    w_in_p, w_vt = _pack_w_in(w_in)
    wq_p, wk_p, wmvt_p = _pack_mla(mla_w_qb, mla_w_kvb)
    wg_p = _pack_gates(lru_w_gates)
    w_out_b = w_out.astype(BF16)
    rows = seq // GRID_W
    tabs = (_rope_tables(rows, HEAD_DIM, 0, HEAD_DIM, tm_mix)
            + _rope_tables(rows, MLA_ROPE, MLA_NOPE, LANES, tm_mix))

    xs = jnp.concatenate([x.reshape(n_lat, d), ctx.reshape(n_ctx, d)], axis=0)

    for layer in range(depth):
        last = layer == depth - 1
        lam_init = 0.8 - 0.6 * math.exp(-0.3 * layer)
        mod3 = mod[layer].reshape(mod_rows, 1, N_MOD * d)
        g = norm_g[layer][:, None, :]

        xs = _ffn_sublayer(xs, n_all, mod3, 0, g[0], g[1], ffn_w_in, ffn_w_out, layer, 0,
                           rows_fn(tm_ffn), tm_ffn, tf)

        (lru_cols, mq, mk, mvt, dq, dk, dvt, gq, gk, gvt) = _mixer_in(
            xs, mod3, g[2], w_in_p[layer], w_vt[layer], tabs,
            (mla_q_norm[layer][None, :], wq_p[layer], mla_kv_norm[layer][None, :],
             wk_p[layer], wmvt_p[layer]),
            rows_fn(tm_mix), tab_fn(tm_mix), tm_mix)

        bg = lru_b_gates[layer].reshape(2, 1, 2 * GROUP_W)
        hf, hb = _lru_mixer(lru_cols, lru_conv_w[layer], lru_conv_b[layer][None, :],
                            wg_p[layer], bg, lru_lambda[layer][:, None, :],
                            batch, seq, ctx_len, tc)

        diff_extra = [diff_lambda[layer], diff_norm[layer][None, :]]
        diff_k = functools.partial(_diff_kernel, lam_init=lam_init)
        yb = _full_attention(_mla_kernel, mq, mk, mvt, [], batch, seq, ctx_len, tq, None)
        yc = _full_attention(diff_k, dq, dk, dvt, diff_extra, batch, seq, ctx_len, tq, None)
        yd = _gqa_attention(gq, gk, gvt, gqa_sink[layer], batch, seq, ctx_len, None)
        if not last:
            yb = _full_attention(_mla_kernel, mq, mk, mvt, [], batch, seq, ctx_len, tq, yb)
            yc = _full_attention(diff_k, dq, dk, dvt, diff_extra, batch, seq, ctx_len, tq, yc)
            yd = _gqa_attention(gq, gk, gvt, gqa_sink[layer], batch, seq, ctx_len, yd)

        n_out = n_lat if last else n_all
        xs = _mixer_out(xs, n_out, mod3, g[3], hf, hb, lru_cols, yb, yc, yd, w_out_b[layer],
                        rows_fn(tm_mix), tm_mix)
        xs = _ffn_sublayer(xs, n_out, mod3, 6, g[4], g[5], ffn_w_in, ffn_w_out, layer, 1,
                           rows_fn(tm_ffn), tm_ffn, tf)

    return xs[:n_lat].reshape(batch, seq, d)
```

```python
import functools
import math

import jax
import jax.numpy as jnp
from jax import lax
from jax.experimental import pallas as pl
from jax.experimental.pallas import tpu as pltpu

F32 = jnp.float32
BF16 = jnp.bfloat16

GRID_W = 64
HEAD_DIM = 64
N_MOD = 9
GROUP_W = 512
LRU_HEADS = 8
LRU_BLOCK = 64
LRU_C = 8.0
MLA_HEADS = 8
MLA_NOPE = 64
MLA_ROPE = 32
MLA_V = 64
MLA_Q_RANK = 384
MLA_KV_RANK = 256
DIFF_HEADS = 4
DIFF_D = 64
DIFF_V = 128
GQA_Q_HEADS = 8
GQA_KV_HEADS = 2
GQA_GROUP = 4
WINDOW = 128
MACARON_WEIGHT = 0.5
ROPE_BASE = 10000.0
RMS_EPS = 1e-6
NEG_INF = -1e30
LOG2E = 1.4426950408889634

LANES = 128
SUBLANES = 8
BF16_ROWS = 16
MLA_PAD = 128
MLA_VA = MLA_V + BF16_ROWS
DIFF_VA = DIFF_V + BF16_ROWS

COLS_LRU = 2 * GROUP_W
COLS_MLA = MLA_Q_RANK + MLA_KV_RANK + MLA_ROPE
DIFF_QK = 2 * DIFF_HEADS * 2 * DIFF_D
COLS_DIFF = DIFF_QK + DIFF_HEADS * DIFF_V
GQA_QK = (GQA_Q_HEADS + GQA_KV_HEADS) * HEAD_DIM
GQA_KVW = GQA_KV_HEADS * HEAD_DIM
COLS_GQA = GQA_QK + GQA_KVW
P_LRU = 0
P_MLA = COLS_LRU
W_MLA = 768
P_DIFF = P_MLA + W_MLA
P_GQA = P_DIFF + DIFF_QK
N_IN_P = P_GQA + GQA_QK

VMEM_LIMIT = 60 * 1024 * 1024


def _cparams(sem):
    return pltpu.CompilerParams(dimension_semantics=sem, vmem_limit_bytes=VMEM_LIMIT)


def _rms(x, g):
    ms = jnp.mean(x * x, axis=-1, keepdims=True)
    return x * lax.rsqrt(ms + RMS_EPS) * g


def _dot(a, b):
    return jnp.dot(a, b, preferred_element_type=F32)


def _dot_nt(a, b):
    return lax.dot_general(a, b, (((1,), (1,)), ((), ())), preferred_element_type=F32)


def _layer_spec(arr, layer, *lead):
    shape = arr.shape[1 + len(lead):]
    return pl.BlockSpec((None,) * (1 + len(lead)) + shape,
                        lambda *_: (layer,) + lead + (0,) * len(shape),
                        pipeline_mode=pl.Buffered(1))


def _mod_spec(d, layer, row_fn, k):
    return pl.BlockSpec((None, None, 1, d), lambda i, *_: (layer, row_fn(i), 0, k))


def _skip_first(kernel):
    def wrapped(_, *refs, **kw):
        return kernel(*refs, **kw)
    return wrapped


def _mod_kernel(c_ref, w_ref, b_ref, o_ref):
    s = c_ref[...]
    s = s * jax.nn.sigmoid(s)
    o_ref[...] = _dot(s.astype(BF16), w_ref[...].astype(BF16)) + b_ref[...]


def _modulation(cvec, w_mod, b_mod, tn):
    depth, d, n = w_mod.shape
    rows = cvec.shape[0]
    return pl.pallas_call(
        _mod_kernel,
        grid=(depth, n // tn),
        in_specs=[
            pl.BlockSpec((rows, d), lambda l, j: (0, 0)),
            pl.BlockSpec((None, d, tn), lambda l, j: (l, 0, j)),
            pl.BlockSpec((None, 1, tn), lambda l, j: (l, 0, j)),
        ],
        out_specs=pl.BlockSpec((None, rows, tn), lambda l, j: (l, 0, j)),
        out_shape=jax.ShapeDtypeStruct((depth, rows, n), F32),
        compiler_params=_cparams(("arbitrary", "arbitrary")),
        name="modulation",
    )(cvec, w_mod, b_mod.reshape(depth, 1, n))


FFN_ROW_CHUNK = 64


def _ffn_kernel(x_ref, shift_ref, scale_ref, gate_ref, gpre_ref, gpost_ref,
                wg_ref, wu_ref, wo_ref, o_ref, h_ref, *, nj):
    j = pl.program_id(1)
    n_chunks = x_ref.shape[0] // FFN_ROW_CHUNK

    def rows(r):
        return pl.ds(pl.multiple_of(r * FFN_ROW_CHUNK, FFN_ROW_CHUNK), FFN_ROW_CHUNK)

    @pl.when(j == 0)
    def _():
        mul = gpre_ref[...] * (1.0 + scale_ref[...])
        shift = shift_ref[...]

        def body(r, carry):
            x = x_ref[rows(r), :]
            ms = jnp.mean(x * x, axis=-1, keepdims=True)
            h_ref[rows(r), :] = (x * lax.rsqrt(ms + RMS_EPS) * mul + shift).astype(BF16)
            return carry

        lax.fori_loop(0, n_chunks, body, 0, unroll=2)
        o_ref[...] = jnp.zeros_like(o_ref)

    h = h_ref[...]
    g = _dot(h, wg_ref[...].astype(BF16))
    u = _dot(h, wu_ref[...].astype(BF16))
    a = (g * jax.nn.sigmoid(g) * u).astype(BF16)
    o_ref[...] += _dot(a, wo_ref[...].astype(BF16))

    @pl.when(j == nj - 1)
    def _():
        mul = (MACARON_WEIGHT * gate_ref[...]) * gpost_ref[...]

        def body(r, carry):
            y = o_ref[rows(r), :]
            ms = jnp.mean(y * y, axis=-1, keepdims=True)
            o_ref[rows(r), :] = x_ref[rows(r), :] + y * lax.rsqrt(ms + RMS_EPS) * mul
            return carry

        lax.fori_loop(0, n_chunks, body, 0)


def _ffn_sublayer(x_in, n_blocks, out_rows, out_blk0, prev_out, mod4, k0, norm4, gi,
                  w_in_all, w_out_all, layer, sub, row_fn, tm, tf):
    d = x_in.shape[1]
    d_ff = w_out_all.shape[2]
    nj = d_ff // tf
    out_row = lambda i: row_fn(out_blk0 + i)
    g_spec = lambda idx: pl.BlockSpec((None, None, 1, d), lambda i, j: (layer, idx, 0, 0))
    lead_specs, lead_args, aliases, body = [], [], {}, _ffn_kernel
    if prev_out is not None:
        lead_specs, lead_args, aliases = [pl.BlockSpec(memory_space=pl.ANY)], [prev_out], {0: 0}
        body = _skip_first(_ffn_kernel)
    return pl.pallas_call(
        functools.partial(body, nj=nj),
        grid=(n_blocks, nj),
        in_specs=lead_specs + [
            pl.BlockSpec((tm, d), lambda i, j: (i, 0), pipeline_mode=pl.Buffered(1)),
            _mod_spec(d, layer, out_row, k0), _mod_spec(d, layer, out_row, k0 + 1),
            _mod_spec(d, layer, out_row, k0 + 2),
            g_spec(gi), g_spec(gi + 1),
            pl.BlockSpec((None, None, d, tf), lambda i, j: (layer, sub, 0, j)),
            pl.BlockSpec((None, None, d, tf), lambda i, j: (layer, sub, 0, nj + j)),
            pl.BlockSpec((None, None, tf, d), lambda i, j: (layer, sub, j, 0)),
        ],
        out_specs=pl.BlockSpec((tm, d), lambda i, j: (out_blk0 + i, 0)),
        out_shape=jax.ShapeDtypeStruct((out_rows, d), F32),
        scratch_shapes=[pltpu.VMEM((tm, d), BF16)],
        input_output_aliases=aliases,
        compiler_params=_cparams(("arbitrary", "arbitrary")),
        name="ffn_sublayer",
    )(*lead_args, x_in, mod4, mod4, mod4, norm4, norm4, w_in_all, w_in_all, w_out_all)


def _swap_halves(x, first, half):
    return jnp.where(first, pltpu.roll(x, LANES - half, 1), pltpu.roll(x, half, 1))


def _rope_chunks(x, cos, sin, first, half):
    out = []
    for c in range(x.shape[1] // LANES):
        blk = x[:, c * LANES:(c + 1) * LANES]
        out.append(blk * cos + _swap_halves(blk, first, half) * sin)
    return out[0] if len(out) == 1 else jnp.concatenate(out, axis=1)


def _with_ones_rows(vt, heads, rows_per_head):
    ones = jnp.ones((BF16_ROWS, vt.shape[1]), vt.dtype)
    parts = []
    for h in range(heads):
        parts += [vt[h * rows_per_head:(h + 1) * rows_per_head], ones]
    return jnp.concatenate(parts, axis=0)


def _mixer_in_kernel(x_ref, shift_ref, scale_ref, g_ref, w_ref, wvt_ref,
                     cos_hd_ref, sin_hd_ref, cos_mla_ref, sin_mla_ref,
                     qn_ref, wq_ref, kvn_ref, wk_ref, wmvt_ref,
                     lru_ref, mq_ref, mk_ref, mvt_ref, dq_ref, dk_ref, dvt_ref,
                     gq_ref, gk_ref, gvt_ref):
    h = (_rms(x_ref[...], g_ref[...]) * (1.0 + scale_ref[...]) + shift_ref[...]).astype(BF16)
    tm = h.shape[0]
    lane = lax.broadcasted_iota(jnp.int32, (tm, LANES), 1)
    first_hd = (lane % HEAD_DIM) < (HEAD_DIM // 2)
    first_mla = lane < (MLA_NOPE + MLA_ROPE // 2)
    cos_hd, sin_hd = cos_hd_ref[...], sin_hd_ref[...]
    cos_mla, sin_mla = cos_mla_ref[...], sin_mla_ref[...]

    lru_ref[...] = _dot(h, w_ref[:, P_LRU:P_LRU + COLS_LRU])

    cm = _dot(h, w_ref[:, P_MLA:P_MLA + W_MLA])
    qn = _rms(cm[:, :MLA_Q_RANK], qn_ref[...]).astype(BF16)
    q = _rope_chunks(_dot(qn, wq_ref[...]), cos_mla, sin_mla, first_mla, MLA_ROPE // 2)
    mq_ref[...] = (q * (LOG2E * (MLA_NOPE + MLA_ROPE) ** -0.5)).astype(BF16)
    kvn = _rms(cm[:, MLA_Q_RANK:MLA_Q_RANK + MLA_KV_RANK], kvn_ref[...]).astype(BF16)
    k_rope = _rope_chunks(cm[:, W_MLA - LANES:], cos_mla, sin_mla, first_mla, MLA_ROPE // 2)
    k_nope = _dot(kvn, wk_ref[...])
    mk_ref[...] = (k_nope + jnp.concatenate([k_rope] * MLA_HEADS, axis=1)).astype(BF16)
    mvt = _dot_nt(wmvt_ref[...], kvn)
    mvt_ref[...] = _with_ones_rows(mvt, MLA_HEADS, MLA_V).astype(BF16)

    cd = _dot(h, w_ref[:, P_DIFF:P_DIFF + DIFF_QK])
    half = DIFF_QK // 2
    dq_ref[...] = (_rope_chunks(cd[:, :half], cos_hd, sin_hd, first_hd, HEAD_DIM // 2)
                   * (LOG2E * DIFF_D ** -0.5)).astype(BF16)
    dk_ref[...] = _rope_chunks(cd[:, half:], cos_hd, sin_hd, first_hd, HEAD_DIM // 2).astype(BF16)

    nq = GQA_Q_HEADS * HEAD_DIM
    cg = _dot(h, w_ref[:, P_GQA:P_GQA + GQA_QK])
    gq_ref[...] = (_rope_chunks(cg[:, :nq], cos_hd, sin_hd, first_hd, HEAD_DIM // 2)
                   * (LOG2E * HEAD_DIM ** -0.5)).astype(BF16)
    gk_ref[...] = _rope_chunks(cg[:, nq:], cos_hd, sin_hd, first_hd, HEAD_DIM // 2).astype(BF16)

    vt = _dot_nt(wvt_ref[...], h)
    dvt_ref[...] = _with_ones_rows(vt[:GROUP_W], DIFF_HEADS, DIFF_V).astype(BF16)
    gvt_ref[...] = vt[GROUP_W:].astype(BF16)


def _mixer_in(xs, mod4, norm4, w_in_p, w_vt, tabs, mla_w, layer, row_fn, tab_fn, tm):
    n_rows, d = xs.shape
    cos_hd, sin_hd, cos_mla, sin_mla = tabs
    qn, wq, kvn, wk, wmvt = mla_w
    tab_spec = pl.BlockSpec((tm, LANES), lambda i: (tab_fn(i), 0))
    outs = [(COLS_LRU, F32, False),
            (MLA_HEADS * MLA_PAD, BF16, False), (MLA_HEADS * MLA_PAD, BF16, False),
            (MLA_HEADS * MLA_VA, BF16, True),
            (GROUP_W, BF16, False), (GROUP_W, BF16, False), (DIFF_HEADS * DIFF_VA, BF16, True),
            (GROUP_W, BF16, False), (GQA_KVW, BF16, False), (GQA_KVW, BF16, True)]
    out_specs = [pl.BlockSpec((w, tm), lambda i: (0, i)) if t else pl.BlockSpec((tm, w), lambda i: (i, 0))
                 for w, _, t in outs]
    out_shape = [jax.ShapeDtypeStruct((w, n_rows) if t else (n_rows, w), dt) for w, dt, t in outs]
    return pl.pallas_call(
        _mixer_in_kernel,
        grid=(n_rows // tm,),
        in_specs=[
            pl.BlockSpec((tm, d), lambda i: (i, 0)),
            _mod_spec(d, layer, row_fn, 3), _mod_spec(d, layer, row_fn, 4),
            _layer_spec(norm4, layer, 2), _layer_spec(w_in_p, layer), _layer_spec(w_vt, layer),
            tab_spec, tab_spec, tab_spec, tab_spec,
            _layer_spec(qn, layer), _layer_spec(wq, layer), _layer_spec(kvn, layer),
            _layer_spec(wk, layer), _layer_spec(wmvt, layer),
        ],
        out_specs=out_specs,
        out_shape=out_shape,
        compiler_params=_cparams(("arbitrary",)),
        name="mixer_in",
    )(xs, mod4, mod4, norm4, w_in_p, w_vt, cos_hd, sin_hd, cos_mla, sin_mla, qn, wq, kvn, wk, wmvt)


def _sigmoid(x):
    return 0.5 * jnp.tanh(0.5 * x) + 0.5


def _lru_direction(x, prev, nxt, cw, cb, wg, bg, lam, reverse, carry_ref, h_ref, a_scr, b_scr):
    tc, ch = x.shape
    ext = jnp.concatenate([prev, x, nxt], axis=0)
    n_ext = tc + 2 * SUBLANES

    def tap(shift):
        return pltpu.roll(ext, (-shift) % n_ext, 0)[SUBLANES:SUBLANES + tc]

    u = cw[0:1] * tap(-2) + cw[1:2] * tap(-1) + cw[2:3] * x + cw[3:4] * tap(1) + cb
    g = _dot(u.astype(BF16), wg) + bg
    r = _sigmoid(g[:, :ch])
    i = _sigmoid(g[:, ch:])
    log_a = -LRU_C * r * jax.nn.softplus(-lam)
    a = jnp.exp(log_a)
    b = jnp.sqrt((1.0 - a) * (1.0 + a)) * (i * u)

    nt = tc // SUBLANES
    a = a.reshape(nt, SUBLANES, ch)
    b = b.reshape(nt, SUBLANES, ch)
    row = lax.broadcasted_iota(jnp.int32, (nt, SUBLANES, ch), 1)
    for s in (1, 2, 4):
        ok = (row < SUBLANES - s) if reverse else (row >= s)
        shift = SUBLANES - s if reverse else s
        a_sh, b_sh = pltpu.roll(a, shift, 1), pltpu.roll(b, shift, 1)
        b = a * jnp.where(ok, b_sh, 0.0) + b
        a = a * jnp.where(ok, a_sh, 1.0)
    a_scr[...] = a.reshape(tc, ch)
    b_scr[...] = b.reshape(tc, ch)

    def body(k, carry):
        kk = nt - 1 - k if reverse else k
        sl = pl.ds(pl.multiple_of(kk * SUBLANES, SUBLANES), SUBLANES)
        ht = b_scr[sl, :] + a_scr[sl, :] * carry
        h_ref[sl, :] = ht
        edge = ht[0:1, :] if reverse else ht[SUBLANES - 1:SUBLANES, :]
        return jnp.broadcast_to(edge, (SUBLANES, ch))

    carry_ref[...] = lax.fori_loop(0, nt, body, carry_ref[...])


def _lru_kernel(xf_ref, pf_ref, nf_ref, xb_ref, pb_ref, nb_ref,
                cw_ref, cb_ref, wgf_ref, wgb_ref, bgf_ref, bgb_ref, lamf_ref, lamb_ref,
                hf_ref, hb_ref, cf_ref, cbk_ref, a_scr, b_scr, *, halo_fn):
    s = pl.program_id(1)

    @pl.when(s == 0)
    def _():
        cf_ref[...] = jnp.zeros_like(cf_ref)
        cbk_ref[...] = jnp.zeros_like(cbk_ref)

    pvf, nvf, pvb, nvb = halo_fn(s)
    cw, cb = cw_ref[...], cb_ref[...]
    _lru_direction(xf_ref[...], pf_ref[...] * pvf, nf_ref[...] * nvf, cw, cb,
                   wgf_ref[...], bgf_ref[...], lamf_ref[...], False, cf_ref, hf_ref, a_scr, b_scr)
    _lru_direction(xb_ref[...], pb_ref[...] * pvb, nb_ref[...] * nvb, cw, cb,
                   wgb_ref[...], bgb_ref[...], lamb_ref[...], True, cbk_ref, hb_ref, a_scr, b_scr)


def _lru_mixer(lru_cols, conv_w, conv_b3, wg, bg4, lam4, layer, batch, seq, ctx_len, tc):
    n_rows = lru_cols.shape[0]
    ch = GROUP_W
    nl, ncx = seq // tc, ctx_len // tc
    n_steps = nl + ncx
    r8 = tc // SUBLANES
    last8 = n_rows // SUBLANES - 1

    def chunk_f(b, s):
        return jnp.where(s < ncx, batch * nl + b * ncx + s, b * nl + s - ncx)

    def chunk_b(b, s):
        return jnp.where(s < ncx, batch * nl + b * ncx + ncx - 1 - s, b * nl + nl - 1 - (s - ncx))

    def seg_pos(s, reverse):
        in_ctx = s < ncx
        if reverse:
            idx = jnp.where(in_ctx, ncx - 1 - s, nl - 1 - (s - ncx))
        else:
            idx = jnp.where(in_ctx, s, s - ncx)
        return idx, jnp.where(in_ctx, ncx, nl)

    def halo_fn(s):
        out = []
        for reverse in (False, True):
            idx, n = seg_pos(s, reverse)
            out += [(idx > 0).astype(F32), (idx < n - 1).astype(F32)]
        return out

    def x_spec(fn):
        return pl.BlockSpec((tc, ch), lambda b, s: (fn(b, s), 0))

    def prev_spec(fn):
        return pl.BlockSpec((SUBLANES, ch), lambda b, s: (jnp.maximum(fn(b, s) * r8 - 1, 0), 0))

    def next_spec(fn):
        return pl.BlockSpec((SUBLANES, ch), lambda b, s: (jnp.minimum((fn(b, s) + 1) * r8, last8), 0))

    return pl.pallas_call(
        functools.partial(_lru_kernel, halo_fn=halo_fn),
        grid=(batch, n_steps),
        in_specs=[
            x_spec(chunk_f), prev_spec(chunk_f), next_spec(chunk_f),
            x_spec(chunk_b), prev_spec(chunk_b), next_spec(chunk_b),
            _layer_spec(conv_w, layer), _layer_spec(conv_b3, layer),
            _layer_spec(wg, layer, 0), _layer_spec(wg, layer, 1),
            _layer_spec(bg4, layer, 0), _layer_spec(bg4, layer, 1),
            _layer_spec(lam4, layer, 0), _layer_spec(lam4, layer, 1),
        ],
        out_specs=[x_spec(chunk_f), x_spec(chunk_b)],
        out_shape=[jax.ShapeDtypeStruct((n_rows, ch), F32)] * 2,
        scratch_shapes=[pltpu.VMEM((SUBLANES, ch), F32), pltpu.VMEM((SUBLANES, ch), F32),
                        pltpu.VMEM((tc, ch), F32), pltpu.VMEM((tc, ch), F32)],
        compiler_params=_cparams(("arbitrary", "arbitrary")),
        name="rglru",
    )(lru_cols, lru_cols, lru_cols, lru_cols, lru_cols, lru_cols,
      conv_w, conv_b3, wg, wg, bg4, bg4, lam4, lam4)


def _softmax_cols(scores, extra=None):
    m = scores[0].max(axis=0, keepdims=True)
    for s in scores[1:]:
        m = jnp.maximum(m, s.max(axis=0, keepdims=True))
    if extra is not None:
        m = jnp.maximum(m, extra)
    ps = [jnp.exp2(s - m) for s in scores]
    den = ps[0].sum(axis=0, keepdims=True)
    for p in ps[1:]:
        den = den + p.sum(axis=0, keepdims=True)
    if extra is not None:
        den = den + jnp.exp2(extra - m)
    return ps, den


def _two_block_queries(q2, width):
    lane = lax.broadcasted_iota(jnp.int32, q2.shape, 1)
    zero = jnp.zeros((), q2.dtype)
    return jnp.concatenate([jnp.where(lane < width, q2, zero), jnp.where(lane >= width, q2, zero)], axis=0)


def _paired_probs(q_ref, ks, width, n_iter, finish, s_bufs):
    def issue(i):
        sl = slice(2 * i * width, 2 * (i + 1) * width)
        qq = _two_block_queries(q_ref[:, sl], width)
        buf, row, m = s_bufs[i % 2], 0, None
        for k in ks:
            s = _dot_nt(k[:, sl], qq)
            buf[row:row + k.shape[0], :] = s
            row += k.shape[0]
            ms = s.max(axis=0, keepdims=True)
            m = ms if m is None else jnp.maximum(m, ms)
        return m

    m_next = issue(0)
    for i in range(n_iter):
        m = m_next
        if i + 1 < n_iter:
            m_next = issue(i + 1)
        finish(i, jnp.exp2(s_bufs[i % 2][...] - m).astype(BF16))


def _weighted_values(vts, rows, p):
    o, row = None, 0
    for vt in vts:
        t = _dot(vt[rows, :], p[row:row + vt.shape[1], :])
        o = t if o is None else o + t
        row += vt.shape[1]
    return o


def _mla_kernel(q_ref, *refs, n_seg):
    ks, vts = refs[:n_seg], refs[n_seg:2 * n_seg]
    o_ref, s_bufs = refs[2 * n_seg], refs[2 * n_seg + 1:]
    tq = q_ref.shape[0]
    outs = []

    def finish(hp, p):
        for e in range(2):
            h = 2 * hp + e
            oa = _weighted_values(vts, slice(h * MLA_VA, (h + 1) * MLA_VA), p[:, e * tq:(e + 1) * tq])
            outs.append(oa[:MLA_V] / oa[MLA_V:MLA_V + 1])

    _paired_probs(q_ref, ks, MLA_PAD, MLA_HEADS // 2, finish, s_bufs)
    o_ref[...] = jnp.concatenate(outs, axis=0).T.astype(o_ref.dtype)


def _diff_kernel(q_ref, lv_ref, g_ref, *refs, n_seg, lam_init):
    ks, vts = refs[:n_seg], refs[n_seg:2 * n_seg]
    o_ref, s_bufs = refs[2 * n_seg], refs[2 * n_seg + 1:]
    lv = lv_ref[...]
    lam = (jnp.exp(jnp.sum(lv[0:1] * lv[1:2], axis=-1, keepdims=True))
           - jnp.exp(jnp.sum(lv[2:3] * lv[3:4], axis=-1, keepdims=True)) + lam_init)
    tq = q_ref.shape[0]
    outs = []

    def finish(h, p):
        oa = _weighted_values(vts, slice(h * DIFF_VA, (h + 1) * DIFF_VA), p)
        o1 = oa[:DIFF_V, :tq] / oa[DIFF_V:DIFF_V + 1, :tq]
        o2 = oa[:DIFF_V, tq:] * (lam / oa[DIFF_V:DIFF_V + 1, tq:])
        outs.append(o1 - o2)

    _paired_probs(q_ref, ks, DIFF_D, DIFF_HEADS, finish, s_bufs)
    o = jnp.concatenate(outs, axis=0).T
    g = g_ref[...]
    for h in range(DIFF_HEADS):
        sl = slice(h * DIFF_V, (h + 1) * DIFF_V)
        o_ref[:, sl] = (_rms(o[:, sl], g) * (1.0 - lam_init)).astype(o_ref.dtype)


def _full_attention(kernel, q, k, vt, extra_specs, extra, batch, seq, ctx_len, tq, prev_out):
    n_rows = q.shape[0]
    qw, kw, vw = q.shape[1], k.shape[1], vt.shape[0]
    ctx_blk0 = batch * seq // ctx_len

    ctx_k = pl.BlockSpec((ctx_len, kw), lambda b, i: (ctx_blk0 + b, 0))
    ctx_v = pl.BlockSpec((vw, ctx_len), lambda b, i: (0, ctx_blk0 + b))
    lat_k = pl.BlockSpec((seq, kw), lambda b, i: (b, 0))
    lat_v = pl.BlockSpec((vw, seq), lambda b, i: (0, b))
    if prev_out is None:
        nq = seq // tq
        q_map = lambda b, i: (b * nq + i, 0)
        kv_specs, kv_args, n_seg = [lat_k, ctx_k, lat_v, ctx_v], [k, k, vt, vt], 2
        lead_specs, lead_args, aliases = [], [], {}
        body = functools.partial(kernel, n_seg=n_seg)
    else:
        nq = ctx_len // tq
        q_map = lambda b, i: (batch * seq // tq + b * nq + i, 0)
        kv_specs, kv_args, n_seg = [ctx_k, ctx_v], [k, vt], 1
        lead_specs, lead_args, aliases = [pl.BlockSpec(memory_space=pl.ANY)], [prev_out], {0: 0}
        body = functools.partial(_skip_first(kernel), n_seg=n_seg)
    n_keys = seq + ctx_len if prev_out is None else ctx_len
    return pl.pallas_call(
        body,
        grid=(batch, nq),
        in_specs=lead_specs + [pl.BlockSpec((tq, qw), q_map)] + extra_specs + kv_specs,
        out_specs=pl.BlockSpec((tq, GROUP_W), q_map),
        out_shape=jax.ShapeDtypeStruct((n_rows, GROUP_W), BF16),
        scratch_shapes=[pltpu.VMEM((n_keys, 2 * tq), F32)] * 2,
        input_output_aliases=aliases,
        compiler_params=_cparams(("arbitrary", "arbitrary")),
        name="full_attention",
    )(*lead_args, q, *extra, *kv_args)


def _gqa_group(q_ref, kv, ks, vts, masks, sinks, outs):
    ksl = slice(kv * HEAD_DIM, (kv + 1) * HEAD_DIM)
    heads = range(kv * GQA_GROUP, (kv + 1) * GQA_GROUP)
    nq = q_ref.shape[0]
    q4 = jnp.concatenate([q_ref[:, h * HEAD_DIM:(h + 1) * HEAD_DIM] for h in heads], axis=0)
    scores = []
    for k, mask in zip(ks, masks):
        s = _dot_nt(k[:, ksl], q4)
        scores.append(s if mask is None else jnp.where(mask, s, NEG_INF))
    sink = jnp.concatenate([jnp.full((1, nq), sinks(h) * LOG2E, F32) for h in heads], axis=1)
    ps, den = _softmax_cols(scores, extra=sink)
    o = None
    for p, vt in zip(ps, vts):
        t = _dot(vt[ksl, :], p.astype(BF16))
        o = t if o is None else o + t
    o = o / den
    outs += [o[:, g * nq:(g + 1) * nq] for g in range(GQA_GROUP)]


def _gqa_lat_kernel(sink_ref, q_ref, kl_ref, vtl_ref, kc_ref, vtc_ref, o_ref, *, seq, layer):
    n = pl.program_id(1)
    span = 3 * WINDOW
    start = pl.multiple_of(jnp.clip((n - 1) * WINDOW, 0, seq - span), WINDOW)
    kw = kl_ref[pl.ds(start, span), :]
    vtw = vtl_ref[:, pl.ds(start, span)]
    kpos = start + lax.broadcasted_iota(jnp.int32, (span, GQA_GROUP * WINDOW), 0)
    qpos = n * WINDOW + lax.broadcasted_iota(jnp.int32, (span, GQA_GROUP * WINDOW), 1) % WINDOW
    valid = jnp.abs(kpos - qpos) <= WINDOW
    sinks = lambda h: sink_ref[layer, h]
    outs = []
    for kv in range(GQA_KV_HEADS):
        _gqa_group(q_ref, kv, [kc_ref[...], kw], [vtc_ref[...], vtw], [None, valid], sinks, outs)
    o_ref[...] = jnp.concatenate(outs, axis=0).T.astype(o_ref.dtype)


def _gqa_ctx_kernel(_, sink_ref, q_ref, kc_ref, vtc_ref, o_ref, *, layer):
    sinks = lambda h: sink_ref[layer, h]
    outs = []
    for kv in range(GQA_KV_HEADS):
        _gqa_group(q_ref, kv, [kc_ref[...]], [vtc_ref[...]], [None], sinks, outs)
    o_ref[...] = jnp.concatenate(outs, axis=0).T.astype(o_ref.dtype)


def _gqa_attention(q, k, vt, sink, layer, batch, seq, ctx_len, prev_out):
    n_rows = q.shape[0]
    ctx_blk0 = batch * seq // ctx_len
    tq = WINDOW
    smem = pl.BlockSpec(memory_space=pltpu.SMEM)
    ctx_k = pl.BlockSpec((ctx_len, GQA_KVW), lambda b, i: (ctx_blk0 + b, 0))
    ctx_v = pl.BlockSpec((GQA_KVW, ctx_len), lambda b, i: (0, ctx_blk0 + b))
    if prev_out is None:
        nq = seq // tq
        q_map = lambda b, i: (b * nq + i, 0)
        lat_k = pl.BlockSpec((seq, GQA_KVW), lambda b, i: (b, 0))
        lat_v = pl.BlockSpec((GQA_KVW, seq), lambda b, i: (0, b))
        kernel = functools.partial(_gqa_lat_kernel, seq=seq, layer=layer)
        specs = [smem, pl.BlockSpec((tq, GROUP_W), q_map), lat_k, lat_v, ctx_k, ctx_v]
        args, aliases = [sink, q, k, vt, k, vt], {}
    else:
        nq = ctx_len // tq
        q_map = lambda b, i: (batch * seq // tq + b * nq + i, 0)
        kernel = functools.partial(_gqa_ctx_kernel, layer=layer)
        specs = [pl.BlockSpec(memory_space=pl.ANY), smem, pl.BlockSpec((tq, GROUP_W), q_map), ctx_k, ctx_v]
        args, aliases = [prev_out, sink, q, k, vt], {0: 0}
    return pl.pallas_call(
        kernel,
        grid=(batch, nq),
        in_specs=specs,
        out_specs=pl.BlockSpec((tq, GROUP_W), q_map),
        out_shape=jax.ShapeDtypeStruct((n_rows, GROUP_W), BF16),
        input_output_aliases=aliases,
        compiler_params=_cparams(("arbitrary", "arbitrary")),
        name="gqa_attention",
    )(*args)


def _mixer_out_kernel(x_ref, gate_ref, g_ref, hf_ref, hb_ref, lg_ref, yb_ref, yc_ref, yd_ref,
                      w_ref, o_ref):
    ya = ((hf_ref[...] + hb_ref[...]) * jax.nn.gelu(lg_ref[...])).astype(BF16)
    y = _dot(ya, w_ref[0:GROUP_W, :])
    y += _dot(yb_ref[...], w_ref[GROUP_W:2 * GROUP_W, :])
    y += _dot(yc_ref[...], w_ref[2 * GROUP_W:3 * GROUP_W, :])
    y += _dot(yd_ref[...], w_ref[3 * GROUP_W:4 * GROUP_W, :])
    o_ref[...] = x_ref[...] + gate_ref[...] * _rms(y, g_ref[...])


def _mixer_out(xs, n_rows, mod4, norm4, hf, hb, lru_cols, yb, yc, yd, w_out, layer, row_fn, tm):
    d = xs.shape[1]
    row_spec = pl.BlockSpec((tm, GROUP_W), lambda i: (i, 0))
    return pl.pallas_call(
        _mixer_out_kernel,
        grid=(n_rows // tm,),
        in_specs=[
            pl.BlockSpec((tm, d), lambda i: (i, 0)),
            _mod_spec(d, layer, row_fn, 5), _layer_spec(norm4, layer, 3),
            row_spec, row_spec,
            pl.BlockSpec((tm, GROUP_W), lambda i: (i, 1)),
            row_spec, row_spec, row_spec,
            _layer_spec(w_out, layer),
        ],
        out_specs=pl.BlockSpec((tm, d), lambda i: (i, 0)),
        out_shape=jax.ShapeDtypeStruct((n_rows, d), F32),
        compiler_params=_cparams(("arbitrary",)),
        name="mixer_out",
    )(xs, mod4, norm4, hf, hb, lru_cols, yb, yc, yd, w_out)


def _pack_w_in(w_in):
    depth, d, _ = w_in.shape
    o_mla = COLS_LRU
    o_diff = o_mla + COLS_MLA
    o_gqa = o_diff + COLS_DIFF
    z = lambda n: jnp.zeros((depth, d, n), w_in.dtype)
    lat = MLA_Q_RANK + MLA_KV_RANK
    parts = [
        w_in[:, :, :COLS_LRU],
        w_in[:, :, o_mla:o_mla + lat], z(MLA_NOPE),
        w_in[:, :, o_mla + lat:o_diff], z(MLA_PAD - MLA_NOPE - MLA_ROPE),
        w_in[:, :, o_diff:o_diff + DIFF_QK],
        w_in[:, :, o_gqa:o_gqa + GQA_QK],
    ]
    vals = jnp.concatenate([w_in[:, :, o_diff + DIFF_QK:o_gqa], w_in[:, :, o_gqa + GQA_QK:]], axis=-1)
    return jnp.concatenate(parts, axis=-1).astype(BF16), jnp.swapaxes(vals, 1, 2).astype(BF16)


def _pack_mla(w_qb, w_kvb):
    depth = w_qb.shape[0]
    hd = MLA_NOPE + MLA_ROPE
    wq = w_qb.reshape(depth, MLA_Q_RANK, MLA_HEADS, hd)
    wq = jnp.pad(wq, ((0, 0), (0, 0), (0, 0), (0, MLA_PAD - hd)))
    wkv = w_kvb.reshape(depth, MLA_KV_RANK, MLA_HEADS, MLA_NOPE + MLA_V)
    wk = jnp.pad(wkv[..., :MLA_NOPE], ((0, 0), (0, 0), (0, 0), (0, MLA_PAD - MLA_NOPE)))
    wv = wkv[..., MLA_NOPE:].reshape(depth, MLA_KV_RANK, MLA_HEADS * MLA_V)
    return (wq.reshape(depth, MLA_Q_RANK, MLA_HEADS * MLA_PAD).astype(BF16),
            wk.reshape(depth, MLA_KV_RANK, MLA_HEADS * MLA_PAD).astype(BF16),
            jnp.swapaxes(wv, 1, 2).astype(BF16))


def _pack_gates(w_gates):
    depth = w_gates.shape[0]
    eye = jnp.eye(LRU_HEADS, dtype=w_gates.dtype)
    dense = jnp.einsum('ldghij,hk->ldghikj', w_gates, eye).reshape(depth, 2, 2, GROUP_W, GROUP_W)
    return jnp.concatenate([dense[:, :, 0], dense[:, :, 1]], axis=-1).astype(BF16)


def _rope_tables(rows, rot_dim, lane0, period, pad_rows):
    row = jnp.repeat(jnp.arange(rows, dtype=F32), GRID_W)
    col = jnp.tile(jnp.arange(GRID_W, dtype=F32), rows)
    per_axis = rot_dim // 2
    inv = ROPE_BASE ** (-jnp.arange(0, per_axis, 2, dtype=F32) / per_axis)
    ang = jnp.concatenate([row[:, None] * inv, col[:, None] * inv], axis=-1)
    cos, sin = jnp.cos(ang), jnp.sin(ang)
    lane = jnp.arange(LANES)
    rel = (lane - lane0) % period
    active = (lane >= lane0) & (rel < rot_dim)
    idx = rel % per_axis
    sign = jnp.where(rel < per_axis, -1.0, 1.0)
    cos_t = jnp.where(active[None, :], cos[:, idx], 1.0)
    sin_t = jnp.where(active[None, :], sin[:, idx] * sign[None, :], 0.0)
    cos_t = jnp.concatenate([cos_t, jnp.ones((pad_rows, LANES), F32)], axis=0)
    sin_t = jnp.concatenate([sin_t, jnp.zeros((pad_rows, LANES), F32)], axis=0)
    return cos_t, sin_t


def _pick(n, prefs):
    for p in prefs:
        if n % p == 0:
            return p
    return n


def kernel(x, c, ctx, c_ctx, w_mod, b_mod, norm_g, ffn_w_in, ffn_w_out, w_in, w_out,
           lru_conv_w, lru_conv_b, lru_w_gates, lru_b_gates, lru_lambda,
           mla_q_norm, mla_w_qb, mla_kv_norm, mla_w_kvb, diff_lambda, diff_norm, gqa_sink):
    batch, seq, d = x.shape
    ctx_len = ctx.shape[1]
    depth = w_mod.shape[0]
    d_ff = ffn_w_out.shape[2]
    n_lat, n_ctx = batch * seq, batch * ctx_len
    n_all = n_lat + n_ctx

    tm_ffn = _pick(math.gcd(seq, n_ctx), (1024, 512, 256))
    tf = _pick(d_ff, (256, 128))
    tm_mix = _pick(math.gcd(seq, n_ctx), (512, 256, 128))
    tq = _pick(math.gcd(seq, ctx_len), (256, 128))
    tc = _pick(math.gcd(seq, ctx_len), (256, 128))
    tn_mod = _pick(N_MOD * d, (1024, 512, 256))
    assert seq % GRID_W == 0 and seq % WINDOW == 0 and seq >= 3 * WINDOW and ctx_len % WINDOW == 0

    def rows_fn(tm):
        per, nlb = seq // tm, n_lat // tm
        return lambda i: jnp.where(i < nlb, i // per, batch)

    def tab_fn(tm):
        per, nlb = seq // tm, n_lat // tm
        return lambda i: jnp.where(i < nlb, i % per, per)

    mod_rows = SUBLANES * ((batch + 1 + SUBLANES - 1) // SUBLANES)
    cvec = jnp.concatenate([c, c_ctx[None, :], jnp.zeros((mod_rows - batch - 1, d), F32)], axis=0)
    mod4 = _modulation(cvec, w_mod, b_mod, tn_mod).reshape(depth, mod_rows, 1, N_MOD * d)

    w_in_p, w_vt = _pack_w_in(w_in)
    wq_p, wk_p, wmvt_p = _pack_mla(mla_w_qb, mla_w_kvb)
    mla_w = (mla_q_norm[:, None, :], wq_p, mla_kv_norm[:, None, :], wk_p, wmvt_p)
    wg_p = _pack_gates(lru_w_gates)
    w_out_b = w_out.astype(BF16)
    norm4 = norm_g[:, :, None, :]
    conv_b3 = lru_conv_b[:, None, :]
    bg4 = lru_b_gates.reshape(depth, 2, 1, 2 * GROUP_W)
    lam4 = lru_lambda[:, :, None, :]
    diff_g3 = diff_norm[:, None, :]
    rows = seq // GRID_W
    tabs = (_rope_tables(rows, HEAD_DIM, 0, HEAD_DIM, tm_mix)
            + _rope_tables(rows, MLA_ROPE, MLA_NOPE, LANES, tm_mix))

    nb_lat, nb_ctx = n_lat // tm_ffn, n_ctx // tm_ffn
    xs = None
    for layer in range(depth):
        last = layer == depth - 1
        lam_init = 0.8 - 0.6 * math.exp(-0.3 * layer)
        ffn = functools.partial(_ffn_sublayer, w_in_all=ffn_w_in, w_out_all=ffn_w_out, layer=layer,
                                row_fn=rows_fn(tm_ffn), tm=tm_ffn, tf=tf, mod4=mod4, norm4=norm4)

        if layer == 0:
            xs = ffn(x.reshape(n_lat, d), nb_lat, n_all, 0, None, k0=0, gi=0, sub=0)
            xs = ffn(ctx.reshape(n_ctx, d), nb_ctx, n_all, nb_lat, xs, k0=0, gi=0, sub=0)
        else:
            xs = ffn(xs, nb_lat + nb_ctx, n_all, 0, None, k0=0, gi=0, sub=0)

        (lru_cols, mq, mk, mvt, dq, dk, dvt, gq, gk, gvt) = _mixer_in(
            xs, mod4, norm4, w_in_p, w_vt, tabs, mla_w, layer, rows_fn(tm_mix), tab_fn(tm_mix), tm_mix)

        hf, hb = _lru_mixer(lru_cols, lru_conv_w, conv_b3, wg_p, bg4, lam4, layer,
                            batch, seq, ctx_len, tc)

        diff_extra = [diff_lambda, diff_g3]
        diff_specs = [_layer_spec(diff_lambda, layer), _layer_spec(diff_g3, layer)]
        diff_k = functools.partial(_diff_kernel, lam_init=lam_init)
        yb = _full_attention(_mla_kernel, mq, mk, mvt, [], [], batch, seq, ctx_len, tq, None)
        yc = _full_attention(diff_k, dq, dk, dvt, diff_specs, diff_extra, batch, seq, ctx_len, tq, None)
        yd = _gqa_attention(gq, gk, gvt, gqa_sink, layer, batch, seq, ctx_len, None)
        if not last:
            yb = _full_attention(_mla_kernel, mq, mk, mvt, [], [], batch, seq, ctx_len, tq, yb)
            yc = _full_attention(diff_k, dq, dk, dvt, diff_specs, diff_extra, batch, seq, ctx_len, tq, yc)
            yd = _gqa_attention(gq, gk, gvt, gqa_sink, layer, batch, seq, ctx_len, yd)

        n_out = n_lat if last else n_all
        xs = _mixer_out(xs, n_out, mod4, norm4, hf, hb, lru_cols, yb, yc, yd, w_out_b, layer,
                        rows_fn(tm_mix), tm_mix)
        xs = ffn(xs, n_out // tm_ffn, n_out, 0, None, k0=6, gi=4, sub=1)

    return xs.reshape(batch, seq, d)
```

```python
import functools
import math

import jax
import jax.numpy as jnp
from jax import lax
from jax.experimental import pallas as pl
from jax.experimental.pallas import tpu as pltpu

F32 = jnp.float32
BF16 = jnp.bfloat16

GRID_W = 64
HEAD_DIM = 64
N_MOD = 9
GROUP_W = 512
LRU_HEADS = 8
LRU_BLOCK = 64
LRU_C = 8.0
MLA_HEADS = 8
MLA_NOPE = 64
MLA_ROPE = 32
MLA_V = 64
MLA_Q_RANK = 384
MLA_KV_RANK = 256
DIFF_HEADS = 4
DIFF_D = 64
DIFF_V = 128
GQA_Q_HEADS = 8
GQA_KV_HEADS = 2
GQA_GROUP = 4
WINDOW = 128
MACARON_WEIGHT = 0.5
ROPE_BASE = 10000.0
RMS_EPS = 1e-6
NEG_INF = -1e30
LOG2E = 1.4426950408889634

LANES = 128
SUBLANES = 8
MLA_PAD = 128

COLS_LRU = 2 * GROUP_W
COLS_MLA = MLA_Q_RANK + MLA_KV_RANK + MLA_ROPE
DIFF_QK = 2 * DIFF_HEADS * 2 * DIFF_D
COLS_DIFF = DIFF_QK + DIFF_HEADS * DIFF_V
GQA_QK = (GQA_Q_HEADS + GQA_KV_HEADS) * HEAD_DIM
GQA_KVW = GQA_KV_HEADS * HEAD_DIM
COLS_GQA = GQA_QK + GQA_KVW
P_LRU = 0
P_MLA = COLS_LRU
W_MLA = 768
P_DIFF = P_MLA + W_MLA
P_GQA = P_DIFF + DIFF_QK
N_IN_P = P_GQA + GQA_QK

VMEM_LIMIT = 60 * 1024 * 1024


def _cparams(sem):
    return pltpu.CompilerParams(dimension_semantics=sem, vmem_limit_bytes=VMEM_LIMIT)


def _rms(x, g):
    ms = jnp.mean(x * x, axis=-1, keepdims=True)
    return x * lax.rsqrt(ms + RMS_EPS) * g


def _dot(a, b):
    return jnp.dot(a, b, preferred_element_type=F32)


def _dot_nt(a, b):
    return lax.dot_general(a, b, (((1,), (1,)), ((), ())), preferred_element_type=F32)


def _layer_spec(arr, layer, *lead):
    shape = arr.shape[1 + len(lead):]
    return pl.BlockSpec((None,) * (1 + len(lead)) + shape,
                        lambda *_: (layer,) + lead + (0,) * len(shape),
                        pipeline_mode=pl.Buffered(1))


def _mod_spec(d, layer, row_fn, k):
    return pl.BlockSpec((None, None, 1, d), lambda i, *_: (layer, row_fn(i), 0, k))


def _skip_first(kernel):
    def wrapped(_, *refs, **kw):
        return kernel(*refs, **kw)
    return wrapped


def _mod_kernel(c_ref, w_ref, b_ref, o_ref):
    s = c_ref[...]
    s = s * jax.nn.sigmoid(s)
    o_ref[...] = _dot(s.astype(BF16), w_ref[...].astype(BF16)) + b_ref[...]


def _modulation(cvec, w_mod, b_mod, tn):
    depth, d, n = w_mod.shape
    rows = cvec.shape[0]
    return pl.pallas_call(
        _mod_kernel,
        grid=(depth, n // tn),
        in_specs=[
            pl.BlockSpec((rows, d), lambda l, j: (0, 0)),
            pl.BlockSpec((None, d, tn), lambda l, j: (l, 0, j)),
            pl.BlockSpec((None, 1, tn), lambda l, j: (l, 0, j)),
        ],
        out_specs=pl.BlockSpec((None, rows, tn), lambda l, j: (l, 0, j)),
        out_shape=jax.ShapeDtypeStruct((depth, rows, n), F32),
        compiler_params=_cparams(("arbitrary", "arbitrary")),
        name="modulation",
    )(cvec, w_mod, b_mod.reshape(depth, 1, n))


FFN_ROW_CHUNK = 64


def _ffn_kernel(x_ref, shift_ref, scale_ref, gate_ref, gpre_ref, gpost_ref,
                wg_ref, wu_ref, wo_ref, o_ref, h_ref, *, nj):
    j = pl.program_id(1)
    n_chunks = x_ref.shape[0] // FFN_ROW_CHUNK

    def rows(r):
        return pl.ds(pl.multiple_of(r * FFN_ROW_CHUNK, FFN_ROW_CHUNK), FFN_ROW_CHUNK)

    @pl.when(j == 0)
    def _():
        mul = gpre_ref[...] * (1.0 + scale_ref[...])
        shift = shift_ref[...]

        def body(r, carry):
            x = x_ref[rows(r), :]
            ms = jnp.mean(x * x, axis=-1, keepdims=True)
            h_ref[rows(r), :] = (x * lax.rsqrt(ms + RMS_EPS) * mul + shift).astype(BF16)
            return carry

        lax.fori_loop(0, n_chunks, body, 0, unroll=2)
        o_ref[...] = jnp.zeros_like(o_ref)

    h = h_ref[...]
    g = _dot(h, wg_ref[...].astype(BF16))
    u = _dot(h, wu_ref[...].astype(BF16))
    a = (g * jax.nn.sigmoid(g) * u).astype(BF16)
    o_ref[...] += _dot(a, wo_ref[...].astype(BF16))

    @pl.when(j == nj - 1)
    def _():
        mul = (MACARON_WEIGHT * gate_ref[...]) * gpost_ref[...]

        def body(r, carry):
            y = o_ref[rows(r), :]
            ms = jnp.mean(y * y, axis=-1, keepdims=True)
            o_ref[rows(r), :] = x_ref[rows(r), :] + y * lax.rsqrt(ms + RMS_EPS) * mul
            return carry

        lax.fori_loop(0, n_chunks, body, 0)


def _ffn_sublayer(x_in, n_blocks, out_rows, out_blk0, prev_out, mod4, k0, norm4, gi,
                  w_in_all, w_out_all, layer, sub, row_fn, tm, tf):
    d = x_in.shape[1]
    d_ff = w_out_all.shape[2]
    nj = d_ff // tf
    out_row = lambda i: row_fn(out_blk0 + i)
    g_spec = lambda idx: pl.BlockSpec((None, None, 1, d), lambda i, j: (layer, idx, 0, 0))
    lead_specs, lead_args, aliases, body = [], [], {}, _ffn_kernel
    if prev_out is not None:
        lead_specs, lead_args, aliases = [pl.BlockSpec(memory_space=pl.ANY)], [prev_out], {0: 0}
        body = _skip_first(_ffn_kernel)
    return pl.pallas_call(
        functools.partial(body, nj=nj),
        grid=(n_blocks, nj),
        in_specs=lead_specs + [
            pl.BlockSpec((tm, d), lambda i, j: (i, 0), pipeline_mode=pl.Buffered(1)),
            _mod_spec(d, layer, out_row, k0), _mod_spec(d, layer, out_row, k0 + 1),
            _mod_spec(d, layer, out_row, k0 + 2),
            g_spec(gi), g_spec(gi + 1),
            pl.BlockSpec((None, None, d, tf), lambda i, j: (layer, sub, 0, j)),
            pl.BlockSpec((None, None, d, tf), lambda i, j: (layer, sub, 0, nj + j)),
            pl.BlockSpec((None, None, tf, d), lambda i, j: (layer, sub, j, 0)),
        ],
        out_specs=pl.BlockSpec((tm, d), lambda i, j: (out_blk0 + i, 0), pipeline_mode=pl.Buffered(1)),
        out_shape=jax.ShapeDtypeStruct((out_rows, d), F32),
        scratch_shapes=[pltpu.VMEM((tm, d), BF16)],
        input_output_aliases=aliases,
        compiler_params=_cparams(("arbitrary", "arbitrary")),
        name="ffn_sublayer",
    )(*lead_args, x_in, mod4, mod4, mod4, norm4, norm4, w_in_all, w_in_all, w_out_all)


def _swap_halves(x, first, half):
    return jnp.where(first, pltpu.roll(x, LANES - half, 1), pltpu.roll(x, half, 1))


def _rope_chunks(x, cos, sin, first, half):
    out = []
    for c in range(x.shape[1] // LANES):
        blk = x[:, c * LANES:(c + 1) * LANES]
        out.append(blk * cos + _swap_halves(blk, first, half) * sin)
    return out[0] if len(out) == 1 else jnp.concatenate(out, axis=1)


def _mixer_in_kernel(x_ref, shift_ref, scale_ref, g_ref, w_ref, wvt_ref,
                     cos_hd_ref, sin_hd_ref, cos_mla_ref, sin_mla_ref,
                     qn_ref, wq_ref, kvn_ref, wk_ref, wmvt_ref,
                     lru_ref, mq_ref, mk_ref, mvt_ref, dq_ref, dk_ref, dvt_ref,
                     gq_ref, gk_ref, gvt_ref):
    h = (_rms(x_ref[...], g_ref[...]) * (1.0 + scale_ref[...]) + shift_ref[...]).astype(BF16)
    tm = h.shape[0]
    lane = lax.broadcasted_iota(jnp.int32, (tm, LANES), 1)
    first_hd = (lane % HEAD_DIM) < (HEAD_DIM // 2)
    first_mla = lane < (MLA_NOPE + MLA_ROPE // 2)
    cos_hd, sin_hd = cos_hd_ref[...], sin_hd_ref[...]
    cos_mla, sin_mla = cos_mla_ref[...], sin_mla_ref[...]

    lru_ref[...] = _dot(h, w_ref[:, P_LRU:P_LRU + COLS_LRU])

    cm = _dot(h, w_ref[:, P_MLA:P_MLA + W_MLA])
    qn = _rms(cm[:, :MLA_Q_RANK], qn_ref[...]).astype(BF16)
    q = _rope_chunks(_dot(qn, wq_ref[...]), cos_mla, sin_mla, first_mla, MLA_ROPE // 2)
    mq_ref[...] = (q * (LOG2E * (MLA_NOPE + MLA_ROPE) ** -0.5)).astype(BF16)
    kvn = _rms(cm[:, MLA_Q_RANK:MLA_Q_RANK + MLA_KV_RANK], kvn_ref[...]).astype(BF16)
    k_rope = _rope_chunks(cm[:, W_MLA - LANES:], cos_mla, sin_mla, first_mla, MLA_ROPE // 2)
    k_nope = _dot(kvn, wk_ref[...])
    mk_ref[...] = (k_nope + jnp.concatenate([k_rope] * MLA_HEADS, axis=1)).astype(BF16)
    mvt_ref[...] = _dot_nt(wmvt_ref[...], kvn).astype(BF16)

    cd = _dot(h, w_ref[:, P_DIFF:P_DIFF + DIFF_QK])
    half = DIFF_QK // 2
    dq_ref[...] = (_rope_chunks(cd[:, :half], cos_hd, sin_hd, first_hd, HEAD_DIM // 2)
                   * (LOG2E * DIFF_D ** -0.5)).astype(BF16)
    dk_ref[...] = _rope_chunks(cd[:, half:], cos_hd, sin_hd, first_hd, HEAD_DIM // 2).astype(BF16)

    nq = GQA_Q_HEADS * HEAD_DIM
    cg = _dot(h, w_ref[:, P_GQA:P_GQA + GQA_QK])
    gq_ref[...] = (_rope_chunks(cg[:, :nq], cos_hd, sin_hd, first_hd, HEAD_DIM // 2)
                   * (LOG2E * HEAD_DIM ** -0.5)).astype(BF16)
    gk_ref[...] = _rope_chunks(cg[:, nq:], cos_hd, sin_hd, first_hd, HEAD_DIM // 2).astype(BF16)

    vt = _dot_nt(wvt_ref[...], h)
    dvt_ref[...] = vt[:GROUP_W].astype(BF16)
    gvt_ref[...] = vt[GROUP_W:].astype(BF16)


def _mixer_in(xs, mod4, norm4, w_in_p, w_vt, tabs, mla_w, layer, row_fn, tab_fn, tm):
    n_rows, d = xs.shape
    cos_hd, sin_hd, cos_mla, sin_mla = tabs
    qn, wq, kvn, wk, wmvt = mla_w
    tab_spec = pl.BlockSpec((tm, LANES), lambda i: (tab_fn(i), 0))
    outs = [(COLS_LRU, F32, False),
            (MLA_HEADS * MLA_PAD, BF16, False), (MLA_HEADS * MLA_PAD, BF16, False),
            (GROUP_W, BF16, True),
            (GROUP_W, BF16, False), (GROUP_W, BF16, False), (GROUP_W, BF16, True),
            (GROUP_W, BF16, False), (GQA_KVW, BF16, False), (GQA_KVW, BF16, True)]
    out_specs = [pl.BlockSpec((w, tm), lambda i: (0, i)) if t else pl.BlockSpec((tm, w), lambda i: (i, 0))
                 for w, _, t in outs]
    out_shape = [jax.ShapeDtypeStruct((w, n_rows) if t else (n_rows, w), dt) for w, dt, t in outs]
    return pl.pallas_call(
        _mixer_in_kernel,
        grid=(n_rows // tm,),
        in_specs=[
            pl.BlockSpec((tm, d), lambda i: (i, 0)),
            _mod_spec(d, layer, row_fn, 3), _mod_spec(d, layer, row_fn, 4),
            _layer_spec(norm4, layer, 2), _layer_spec(w_in_p, layer), _layer_spec(w_vt, layer),
            tab_spec, tab_spec, tab_spec, tab_spec,
            _layer_spec(qn, layer), _layer_spec(wq, layer), _layer_spec(kvn, layer),
            _layer_spec(wk, layer), _layer_spec(wmvt, layer),
        ],
        out_specs=out_specs,
        out_shape=out_shape,
        compiler_params=_cparams(("arbitrary",)),
        name="mixer_in",
    )(xs, mod4, mod4, norm4, w_in_p, w_vt, cos_hd, sin_hd, cos_mla, sin_mla, qn, wq, kvn, wk, wmvt)


def _sigmoid(x):
    return 0.5 * jnp.tanh(0.5 * x) + 0.5


def _lru_direction(x, prev, nxt, cw, cb, wg, bg, lam, reverse, carry_ref, h_ref, a_scr, b_scr):
    tc, ch = x.shape
    ext = jnp.concatenate([prev, x, nxt], axis=0)
    n_ext = tc + 2 * SUBLANES

    def tap(shift):
        return pltpu.roll(ext, (-shift) % n_ext, 0)[SUBLANES:SUBLANES + tc]

    u = cw[0:1] * tap(-2) + cw[1:2] * tap(-1) + cw[2:3] * x + cw[3:4] * tap(1) + cb
    g = _dot(u.astype(BF16), wg) + bg
    r = _sigmoid(g[:, :ch])
    i = _sigmoid(g[:, ch:])
    log_a = -LRU_C * r * jax.nn.softplus(-lam)
    a = jnp.exp(log_a)
    b = jnp.sqrt((1.0 - a) * (1.0 + a)) * (i * u)

    nt = tc // SUBLANES
    a = a.reshape(nt, SUBLANES, ch)
    b = b.reshape(nt, SUBLANES, ch)
    row = lax.broadcasted_iota(jnp.int32, (nt, SUBLANES, ch), 1)
    for s in (1, 2, 4):
        ok = (row < SUBLANES - s) if reverse else (row >= s)
        shift = SUBLANES - s if reverse else s
        a_sh, b_sh = pltpu.roll(a, shift, 1), pltpu.roll(b, shift, 1)
        b = a * jnp.where(ok, b_sh, 0.0) + b
        a = a * jnp.where(ok, a_sh, 1.0)
    a_scr[...] = a.reshape(tc, ch)
    b_scr[...] = b.reshape(tc, ch)

    def body(k, carry):
        kk = nt - 1 - k if reverse else k
        sl = pl.ds(pl.multiple_of(kk * SUBLANES, SUBLANES), SUBLANES)
        ht = b_scr[sl, :] + a_scr[sl, :] * carry
        h_ref[sl, :] = ht
        edge = ht[0:1, :] if reverse else ht[SUBLANES - 1:SUBLANES, :]
        return jnp.broadcast_to(edge, (SUBLANES, ch))

    carry_ref[...] = lax.fori_loop(0, nt, body, carry_ref[...])


def _lru_kernel(xf_ref, pf_ref, nf_ref, xb_ref, pb_ref, nb_ref,
                cw_ref, cb_ref, wgf_ref, wgb_ref, bgf_ref, bgb_ref, lamf_ref, lamb_ref,
                hf_ref, hb_ref, cf_ref, cbk_ref, a_scr, b_scr, *, halo_fn):
    s = pl.program_id(1)

    @pl.when(s == 0)
    def _():
        cf_ref[...] = jnp.zeros_like(cf_ref)
        cbk_ref[...] = jnp.zeros_like(cbk_ref)

    pvf, nvf, pvb, nvb = halo_fn(s)
    cw, cb = cw_ref[...], cb_ref[...]
    _lru_direction(xf_ref[...], pf_ref[...] * pvf, nf_ref[...] * nvf, cw, cb,
                   wgf_ref[...], bgf_ref[...], lamf_ref[...], False, cf_ref, hf_ref, a_scr, b_scr)
    _lru_direction(xb_ref[...], pb_ref[...] * pvb, nb_ref[...] * nvb, cw, cb,
                   wgb_ref[...], bgb_ref[...], lamb_ref[...], True, cbk_ref, hb_ref, a_scr, b_scr)


def _lru_mixer(lru_cols, conv_w, conv_b3, wg, bg4, lam4, layer, batch, seq, ctx_len, tc):
    n_rows = lru_cols.shape[0]
    ch = GROUP_W
    nl, ncx = seq // tc, ctx_len // tc
    n_steps = nl + ncx
    r8 = tc // SUBLANES
    last8 = n_rows // SUBLANES - 1

    def chunk_f(b, s):
        return jnp.where(s < ncx, batch * nl + b * ncx + s, b * nl + s - ncx)

    def chunk_b(b, s):
        return jnp.where(s < ncx, batch * nl + b * ncx + ncx - 1 - s, b * nl + nl - 1 - (s - ncx))

    def seg_pos(s, reverse):
        in_ctx = s < ncx
        if reverse:
            idx = jnp.where(in_ctx, ncx - 1 - s, nl - 1 - (s - ncx))
        else:
            idx = jnp.where(in_ctx, s, s - ncx)
        return idx, jnp.where(in_ctx, ncx, nl)

    def halo_fn(s):
        out = []
        for reverse in (False, True):
            idx, n = seg_pos(s, reverse)
            out += [(idx > 0).astype(F32), (idx < n - 1).astype(F32)]
        return out

    def x_spec(fn):
        return pl.BlockSpec((tc, ch), lambda b, s: (fn(b, s), 0))

    def prev_spec(fn):
        return pl.BlockSpec((SUBLANES, ch), lambda b, s: (jnp.maximum(fn(b, s) * r8 - 1, 0), 0))

    def next_spec(fn):
        return pl.BlockSpec((SUBLANES, ch), lambda b, s: (jnp.minimum((fn(b, s) + 1) * r8, last8), 0))

    return pl.pallas_call(
        functools.partial(_lru_kernel, halo_fn=halo_fn),
        grid=(batch, n_steps),
        in_specs=[
            x_spec(chunk_f), prev_spec(chunk_f), next_spec(chunk_f),
            x_spec(chunk_b), prev_spec(chunk_b), next_spec(chunk_b),
            _layer_spec(conv_w, layer), _layer_spec(conv_b3, layer),
            _layer_spec(wg, layer, 0), _layer_spec(wg, layer, 1),
            _layer_spec(bg4, layer, 0), _layer_spec(bg4, layer, 1),
            _layer_spec(lam4, layer, 0), _layer_spec(lam4, layer, 1),
        ],
        out_specs=[x_spec(chunk_f), x_spec(chunk_b)],
        out_shape=[jax.ShapeDtypeStruct((n_rows, ch), F32)] * 2,
        scratch_shapes=[pltpu.VMEM((SUBLANES, ch), F32), pltpu.VMEM((SUBLANES, ch), F32),
                        pltpu.VMEM((tc, ch), F32), pltpu.VMEM((tc, ch), F32)],
        compiler_params=_cparams(("arbitrary", "arbitrary")),
        name="rglru",
    )(lru_cols, lru_cols, lru_cols, lru_cols, lru_cols, lru_cols,
      conv_w, conv_b3, wg, wg, bg4, bg4, lam4, lam4)


def _softmax_cols(scores, extra=None):
    m = scores[0].max(axis=0, keepdims=True)
    for s in scores[1:]:
        m = jnp.maximum(m, s.max(axis=0, keepdims=True))
    if extra is not None:
        m = jnp.maximum(m, extra)
    ps = [jnp.exp2(s - m) for s in scores]
    den = ps[0].sum(axis=0, keepdims=True)
    for p in ps[1:]:
        den = den + p.sum(axis=0, keepdims=True)
    if extra is not None:
        den = den + jnp.exp2(extra - m)
    return ps, den


def _two_block_queries(q2, width):
    lane = lax.broadcasted_iota(jnp.int32, q2.shape, 1)
    zero = jnp.zeros((), q2.dtype)
    return jnp.concatenate([jnp.where(lane < width, q2, zero), jnp.where(lane >= width, q2, zero)], axis=0)


def _paired_probs(q_ref, ks, width, n_iter, finish, s_bufs):
    def issue(i):
        sl = slice(2 * i * width, 2 * (i + 1) * width)
        qq = _two_block_queries(q_ref[:, sl], width)
        buf, row, m = s_bufs[i % 2], 0, None
        for k in ks:
            s = _dot_nt(k[:, sl], qq)
            buf[row:row + k.shape[0], :] = s
            row += k.shape[0]
            ms = s.max(axis=0, keepdims=True)
            m = ms if m is None else jnp.maximum(m, ms)
        return m

    m_next = issue(0)
    for i in range(n_iter):
        m = m_next
        if i + 1 < n_iter:
            m_next = issue(i + 1)
        p = jnp.exp2(s_bufs[i % 2][...] - m)
        finish(i, p, p.sum(axis=0, keepdims=True))


def _weighted_values(vts, rows, p):
    o, row = None, 0
    for vt in vts:
        t = _dot(vt[rows, :], p[row:row + vt.shape[1], :])
        o = t if o is None else o + t
        row += vt.shape[1]
    return o


def _mla_kernel(q_ref, *refs, n_seg):
    ks, vts = refs[:n_seg], refs[n_seg:2 * n_seg]
    o_ref, s_bufs = refs[2 * n_seg], refs[2 * n_seg + 1:]
    tq = q_ref.shape[0]
    outs = []

    def finish(hp, p, den):
        p = p.astype(BF16)
        for e in range(2):
            h = 2 * hp + e
            cols = slice(e * tq, (e + 1) * tq)
            o = _weighted_values(vts, slice(h * MLA_V, (h + 1) * MLA_V), p[:, cols])
            outs.append(o / den[:, cols])

    _paired_probs(q_ref, ks, MLA_PAD, MLA_HEADS // 2, finish, s_bufs)
    o_ref[...] = jnp.concatenate(outs, axis=0).T.astype(o_ref.dtype)


def _diff_kernel(q_ref, lv_ref, g_ref, *refs, n_seg, lam_init):
    ks, vts = refs[:n_seg], refs[n_seg:2 * n_seg]
    o_ref, s_bufs = refs[2 * n_seg], refs[2 * n_seg + 1:]
    lv = lv_ref[...]
    lam = (jnp.exp(jnp.sum(lv[0:1] * lv[1:2], axis=-1, keepdims=True))
           - jnp.exp(jnp.sum(lv[2:3] * lv[3:4], axis=-1, keepdims=True)) + lam_init)
    tq = q_ref.shape[0]
    outs = []

    def finish(h, p, den):
        c1 = 1.0 / den[:, :tq]
        c2 = lam / den[:, tq:]
        w = (p[:, :tq] * c1 - p[:, tq:] * c2).astype(BF16)
        outs.append(_weighted_values(vts, slice(h * DIFF_V, (h + 1) * DIFF_V), w))

    _paired_probs(q_ref, ks, DIFF_D, DIFF_HEADS, finish, s_bufs)
    o = jnp.concatenate(outs, axis=0).T
    g = g_ref[...]
    for h in range(DIFF_HEADS):
        sl = slice(h * DIFF_V, (h + 1) * DIFF_V)
        o_ref[:, sl] = (_rms(o[:, sl], g) * (1.0 - lam_init)).astype(o_ref.dtype)


def _full_attention(kernel, q, k, vt, extra_specs, extra, batch, seq, ctx_len, tq, prev_out):
    n_rows = q.shape[0]
    qw, kw, vw = q.shape[1], k.shape[1], vt.shape[0]
    ctx_blk0 = batch * seq // ctx_len

    ctx_k = pl.BlockSpec((ctx_len, kw), lambda b, i: (ctx_blk0 + b, 0))
    ctx_v = pl.BlockSpec((vw, ctx_len), lambda b, i: (0, ctx_blk0 + b))
    lat_k = pl.BlockSpec((seq, kw), lambda b, i: (b, 0))
    lat_v = pl.BlockSpec((vw, seq), lambda b, i: (0, b))
    if prev_out is None:
        nq = seq // tq
        q_map = lambda b, i: (b * nq + i, 0)
        kv_specs, kv_args, n_seg = [lat_k, ctx_k, lat_v, ctx_v], [k, k, vt, vt], 2
        lead_specs, lead_args, aliases = [], [], {}
        body = functools.partial(kernel, n_seg=n_seg)
    else:
        nq = ctx_len // tq
        q_map = lambda b, i: (batch * seq // tq + b * nq + i, 0)
        kv_specs, kv_args, n_seg = [ctx_k, ctx_v], [k, vt], 1
        lead_specs, lead_args, aliases = [pl.BlockSpec(memory_space=pl.ANY)], [prev_out], {0: 0}
        body = functools.partial(_skip_first(kernel), n_seg=n_seg)
    n_keys = seq + ctx_len if prev_out is None else ctx_len
    return pl.pallas_call(
        body,
        grid=(batch, nq),
        in_specs=lead_specs + [pl.BlockSpec((tq, qw), q_map)] + extra_specs + kv_specs,
        out_specs=pl.BlockSpec((tq, GROUP_W), q_map),
        out_shape=jax.ShapeDtypeStruct((n_rows, GROUP_W), BF16),
        scratch_shapes=[pltpu.VMEM((n_keys, 2 * tq), F32)] * 2,
        input_output_aliases=aliases,
        compiler_params=_cparams(("arbitrary", "arbitrary")),
        name="full_attention",
    )(*lead_args, q, *extra, *kv_args)


def _gqa_group(q_ref, kv, ks, vts, masks, sinks, outs):
    ksl = slice(kv * HEAD_DIM, (kv + 1) * HEAD_DIM)
    heads = range(kv * GQA_GROUP, (kv + 1) * GQA_GROUP)
    nq = q_ref.shape[0]
    q4 = jnp.concatenate([q_ref[:, h * HEAD_DIM:(h + 1) * HEAD_DIM] for h in heads], axis=0)
    scores = []
    for k, mask in zip(ks, masks):
        s = _dot_nt(k[:, ksl], q4)
        scores.append(s if mask is None else jnp.where(mask, s, NEG_INF))
    sink = jnp.concatenate([jnp.full((1, nq), sinks(h) * LOG2E, F32) for h in heads], axis=1)
    ps, den = _softmax_cols(scores, extra=sink)
    o = None
    for p, vt in zip(ps, vts):
        t = _dot(vt[ksl, :], p.astype(BF16))
        o = t if o is None else o + t
    o = o / den
    outs += [o[:, g * nq:(g + 1) * nq] for g in range(GQA_GROUP)]


def _gqa_lat_kernel(sink_ref, q_ref, kl_ref, vtl_ref, kc_ref, vtc_ref, o_ref, *, seq, layer):
    n = pl.program_id(1)
    span = 3 * WINDOW
    start = pl.multiple_of(jnp.clip((n - 1) * WINDOW, 0, seq - span), WINDOW)
    kw = kl_ref[pl.ds(start, span), :]
    vtw = vtl_ref[:, pl.ds(start, span)]
    kpos = start + lax.broadcasted_iota(jnp.int32, (span, GQA_GROUP * WINDOW), 0)
    qpos = n * WINDOW + lax.broadcasted_iota(jnp.int32, (span, GQA_GROUP * WINDOW), 1) % WINDOW
    valid = jnp.abs(kpos - qpos) <= WINDOW
    sinks = lambda h: sink_ref[layer, h]
    outs = []
    for kv in range(GQA_KV_HEADS):
        _gqa_group(q_ref, kv, [kc_ref[...], kw], [vtc_ref[...], vtw], [None, valid], sinks, outs)
    o_ref[...] = jnp.concatenate(outs, axis=0).T.astype(o_ref.dtype)


def _gqa_ctx_kernel(_, sink_ref, q_ref, kc_ref, vtc_ref, o_ref, *, layer):
    sinks = lambda h: sink_ref[layer, h]
    outs = []
    for kv in range(GQA_KV_HEADS):
        _gqa_group(q_ref, kv, [kc_ref[...]], [vtc_ref[...]], [None], sinks, outs)
    o_ref[...] = jnp.concatenate(outs, axis=0).T.astype(o_ref.dtype)


def _gqa_attention(q, k, vt, sink, layer, batch, seq, ctx_len, prev_out):
    n_rows = q.shape[0]
    ctx_blk0 = batch * seq // ctx_len
    tq = WINDOW
    smem = pl.BlockSpec(memory_space=pltpu.SMEM)
    ctx_k = pl.BlockSpec((ctx_len, GQA_KVW), lambda b, i: (ctx_blk0 + b, 0))
    ctx_v = pl.BlockSpec((GQA_KVW, ctx_len), lambda b, i: (0, ctx_blk0 + b))
    if prev_out is None:
        nq = seq // tq
        q_map = lambda b, i: (b * nq + i, 0)
        lat_k = pl.BlockSpec((seq, GQA_KVW), lambda b, i: (b, 0))
        lat_v = pl.BlockSpec((GQA_KVW, seq), lambda b, i: (0, b))
        kernel = functools.partial(_gqa_lat_kernel, seq=seq, layer=layer)
        specs = [smem, pl.BlockSpec((tq, GROUP_W), q_map), lat_k, lat_v, ctx_k, ctx_v]
        args, aliases = [sink, q, k, vt, k, vt], {}
    else:
        nq = ctx_len // tq
        q_map = lambda b, i: (batch * seq // tq + b * nq + i, 0)
        kernel = functools.partial(_gqa_ctx_kernel, layer=layer)
        specs = [pl.BlockSpec(memory_space=pl.ANY), smem, pl.BlockSpec((tq, GROUP_W), q_map), ctx_k, ctx_v]
        args, aliases = [prev_out, sink, q, k, vt], {0: 0}
    return pl.pallas_call(
        kernel,
        grid=(batch, nq),
        in_specs=specs,
        out_specs=pl.BlockSpec((tq, GROUP_W), q_map),
        out_shape=jax.ShapeDtypeStruct((n_rows, GROUP_W), BF16),
        input_output_aliases=aliases,
        compiler_params=_cparams(("arbitrary", "arbitrary")),
        name="gqa_attention",
    )(*args)


def _mixer_out_kernel(x_ref, gate_ref, g_ref, hf_ref, hb_ref, lg_ref, yb_ref, yc_ref, yd_ref,
                      w_ref, o_ref):
    ya = ((hf_ref[...] + hb_ref[...]) * jax.nn.gelu(lg_ref[...])).astype(BF16)
    y = _dot(ya, w_ref[0:GROUP_W, :])
    y += _dot(yb_ref[...], w_ref[GROUP_W:2 * GROUP_W, :])
    y += _dot(yc_ref[...], w_ref[2 * GROUP_W:3 * GROUP_W, :])
    y += _dot(yd_ref[...], w_ref[3 * GROUP_W:4 * GROUP_W, :])
    o_ref[...] = x_ref[...] + gate_ref[...] * _rms(y, g_ref[...])


def _mixer_out(xs, n_rows, mod4, norm4, hf, hb, lru_cols, yb, yc, yd, w_out, layer, row_fn, tm):
    d = xs.shape[1]
    row_spec = pl.BlockSpec((tm, GROUP_W), lambda i: (i, 0))
    return pl.pallas_call(
        _mixer_out_kernel,
        grid=(n_rows // tm,),
        in_specs=[
            pl.BlockSpec((tm, d), lambda i: (i, 0)),
            _mod_spec(d, layer, row_fn, 5), _layer_spec(norm4, layer, 3),
            row_spec, row_spec,
            pl.BlockSpec((tm, GROUP_W), lambda i: (i, 1)),
            row_spec, row_spec, row_spec,
            _layer_spec(w_out, layer),
        ],
        out_specs=pl.BlockSpec((tm, d), lambda i: (i, 0)),
        out_shape=jax.ShapeDtypeStruct((n_rows, d), F32),
        compiler_params=_cparams(("arbitrary",)),
        name="mixer_out",
    )(xs, mod4, norm4, hf, hb, lru_cols, yb, yc, yd, w_out)


def _pack_w_in(w_in):
    depth, d, _ = w_in.shape
    o_mla = COLS_LRU
    o_diff = o_mla + COLS_MLA
    o_gqa = o_diff + COLS_DIFF
    z = lambda n: jnp.zeros((depth, d, n), w_in.dtype)
    lat = MLA_Q_RANK + MLA_KV_RANK
    parts = [
        w_in[:, :, :COLS_LRU],
        w_in[:, :, o_mla:o_mla + lat], z(MLA_NOPE),
        w_in[:, :, o_mla + lat:o_diff], z(MLA_PAD - MLA_NOPE - MLA_ROPE),
        w_in[:, :, o_diff:o_diff + DIFF_QK],
        w_in[:, :, o_gqa:o_gqa + GQA_QK],
    ]
    vals = jnp.concatenate([w_in[:, :, o_diff + DIFF_QK:o_gqa], w_in[:, :, o_gqa + GQA_QK:]], axis=-1)
    return jnp.concatenate(parts, axis=-1).astype(BF16), jnp.swapaxes(vals, 1, 2).astype(BF16)


def _pack_mla(w_qb, w_kvb):
    depth = w_qb.shape[0]
    hd = MLA_NOPE + MLA_ROPE
    wq = w_qb.reshape(depth, MLA_Q_RANK, MLA_HEADS, hd)
    wq = jnp.pad(wq, ((0, 0), (0, 0), (0, 0), (0, MLA_PAD - hd)))
    wkv = w_kvb.reshape(depth, MLA_KV_RANK, MLA_HEADS, MLA_NOPE + MLA_V)
    wk = jnp.pad(wkv[..., :MLA_NOPE], ((0, 0), (0, 0), (0, 0), (0, MLA_PAD - MLA_NOPE)))
    wv = wkv[..., MLA_NOPE:].reshape(depth, MLA_KV_RANK, MLA_HEADS * MLA_V)
    return (wq.reshape(depth, MLA_Q_RANK, MLA_HEADS * MLA_PAD).astype(BF16),
            wk.reshape(depth, MLA_KV_RANK, MLA_HEADS * MLA_PAD).astype(BF16),
            jnp.swapaxes(wv, 1, 2).astype(BF16))


def _pack_gates(w_gates):
    depth = w_gates.shape[0]
    eye = jnp.eye(LRU_HEADS, dtype=w_gates.dtype)
    dense = jnp.einsum('ldghij,hk->ldghikj', w_gates, eye).reshape(depth, 2, 2, GROUP_W, GROUP_W)
    return jnp.concatenate([dense[:, :, 0], dense[:, :, 1]], axis=-1).astype(BF16)


def _rope_tables(rows, rot_dim, lane0, period, pad_rows):
    row = jnp.repeat(jnp.arange(rows, dtype=F32), GRID_W)
    col = jnp.tile(jnp.arange(GRID_W, dtype=F32), rows)
    per_axis = rot_dim // 2
    inv = ROPE_BASE ** (-jnp.arange(0, per_axis, 2, dtype=F32) / per_axis)
    ang = jnp.concatenate([row[:, None] * inv, col[:, None] * inv], axis=-1)
    cos, sin = jnp.cos(ang), jnp.sin(ang)
    lane = jnp.arange(LANES)
    rel = (lane - lane0) % period
    active = (lane >= lane0) & (rel < rot_dim)
    idx = rel % per_axis
    sign = jnp.where(rel < per_axis, -1.0, 1.0)
    cos_t = jnp.where(active[None, :], cos[:, idx], 1.0)
    sin_t = jnp.where(active[None, :], sin[:, idx] * sign[None, :], 0.0)
    cos_t = jnp.concatenate([cos_t, jnp.ones((pad_rows, LANES), F32)], axis=0)
    sin_t = jnp.concatenate([sin_t, jnp.zeros((pad_rows, LANES), F32)], axis=0)
    return cos_t, sin_t


def _pick(n, prefs):
    for p in prefs:
        if n % p == 0:
            return p
    return n


def kernel(x, c, ctx, c_ctx, w_mod, b_mod, norm_g, ffn_w_in, ffn_w_out, w_in, w_out,
           lru_conv_w, lru_conv_b, lru_w_gates, lru_b_gates, lru_lambda,
           mla_q_norm, mla_w_qb, mla_kv_norm, mla_w_kvb, diff_lambda, diff_norm, gqa_sink):
    batch, seq, d = x.shape
    ctx_len = ctx.shape[1]
    depth = w_mod.shape[0]
    d_ff = ffn_w_out.shape[2]
    n_lat, n_ctx = batch * seq, batch * ctx_len
    n_all = n_lat + n_ctx

    tm_ffn = _pick(math.gcd(seq, n_ctx), (1024, 512, 256))
    tf = _pick(d_ff, (512, 256, 128))
    tm_mix = _pick(math.gcd(seq, n_ctx), (512, 256, 128))
    tq = _pick(math.gcd(seq, ctx_len), (256, 128))
    tq_lat = _pick(seq, (512, 256, 128))
    tc = _pick(math.gcd(seq, ctx_len), (256, 128))
    tn_mod = _pick(N_MOD * d, (1024, 512, 256))
    assert seq % GRID_W == 0 and seq % WINDOW == 0 and seq >= 3 * WINDOW and ctx_len % WINDOW == 0

    def rows_fn(tm):
        per, nlb = seq // tm, n_lat // tm
        return lambda i: jnp.where(i < nlb, i // per, batch)

    def tab_fn(tm):
        per, nlb = seq // tm, n_lat // tm
        return lambda i: jnp.where(i < nlb, i % per, per)

    mod_rows = SUBLANES * ((batch + 1 + SUBLANES - 1) // SUBLANES)
    cvec = jnp.concatenate([c, c_ctx[None, :], jnp.zeros((mod_rows - batch - 1, d), F32)], axis=0)
    mod4 = _modulation(cvec, w_mod, b_mod, tn_mod).reshape(depth, mod_rows, 1, N_MOD * d)

    w_in_p, w_vt = _pack_w_in(w_in)
    wq_p, wk_p, wmvt_p = _pack_mla(mla_w_qb, mla_w_kvb)
    mla_w = (mla_q_norm[:, None, :], wq_p, mla_kv_norm[:, None, :], wk_p, wmvt_p)
    wg_p = _pack_gates(lru_w_gates)
    w_out_b = w_out.astype(BF16)
    norm4 = norm_g[:, :, None, :]
    conv_b3 = lru_conv_b[:, None, :]
    bg4 = lru_b_gates.reshape(depth, 2, 1, 2 * GROUP_W)
    lam4 = lru_lambda[:, :, None, :]
    diff_g3 = diff_norm[:, None, :]
    rows = seq // GRID_W
    tabs = (_rope_tables(rows, HEAD_DIM, 0, HEAD_DIM, tm_mix)
            + _rope_tables(rows, MLA_ROPE, MLA_NOPE, LANES, tm_mix))

    nb_lat, nb_ctx = n_lat // tm_ffn, n_ctx // tm_ffn
    xs = None
    for layer in range(depth):
        last = layer == depth - 1
        lam_init = 0.8 - 0.6 * math.exp(-0.3 * layer)
        ffn = functools.partial(_ffn_sublayer, w_in_all=ffn_w_in, w_out_all=ffn_w_out, layer=layer,
                                row_fn=rows_fn(tm_ffn), tm=tm_ffn, tf=tf, mod4=mod4, norm4=norm4)

        if layer == 0:
            xs = ffn(x.reshape(n_lat, d), nb_lat, n_all, 0, None, k0=0, gi=0, sub=0)
            xs = ffn(ctx.reshape(n_ctx, d), nb_ctx, n_all, nb_lat, xs, k0=0, gi=0, sub=0)
        else:
            xs = ffn(xs, nb_lat + nb_ctx, n_all, 0, None, k0=0, gi=0, sub=0)

        (lru_cols, mq, mk, mvt, dq, dk, dvt, gq, gk, gvt) = _mixer_in(
            xs, mod4, norm4, w_in_p, w_vt, tabs, mla_w, layer, rows_fn(tm_mix), tab_fn(tm_mix), tm_mix)

        hf, hb = _lru_mixer(lru_cols, lru_conv_w, conv_b3, wg_p, bg4, lam4, layer,
                            batch, seq, ctx_len, tc)

        diff_extra = [diff_lambda, diff_g3]
        diff_specs = [_layer_spec(diff_lambda, layer), _layer_spec(diff_g3, layer)]
        diff_k = functools.partial(_diff_kernel, lam_init=lam_init)
        yb = _full_attention(_mla_kernel, mq, mk, mvt, [], [], batch, seq, ctx_len, tq_lat, None)
        yc = _full_attention(diff_k, dq, dk, dvt, diff_specs, diff_extra, batch, seq, ctx_len, tq_lat, None)
        yd = _gqa_attention(gq, gk, gvt, gqa_sink, layer, batch, seq, ctx_len, None)
        if not last:
            yb = _full_attention(_mla_kernel, mq, mk, mvt, [], [], batch, seq, ctx_len, tq, yb)
            yc = _full_attention(diff_k, dq, dk, dvt, diff_specs, diff_extra, batch, seq, ctx_len, tq, yc)
            yd = _gqa_attention(gq, gk, gvt, gqa_sink, layer, batch, seq, ctx_len, yd)

        n_out = n_lat if last else n_all
        xs = _mixer_out(xs, n_out, mod4, norm4, hf, hb, lru_cols, yb, yc, yd, w_out_b, layer,
                        rows_fn(tm_mix), tm_mix)
        xs = ffn(xs, n_out // tm_ffn, n_out, 0, None, k0=6, gi=4, sub=1)

    return xs.reshape(batch, seq, d)
```

```python
import functools
import math

import jax
import jax.numpy as jnp
from jax import lax
from jax.experimental import pallas as pl
from jax.experimental.pallas import tpu as pltpu

F32 = jnp.float32
BF16 = jnp.bfloat16

GRID_W = 64
HEAD_DIM = 64
N_MOD = 9
GROUP_W = 512
LRU_HEADS = 8
LRU_BLOCK = 64
LRU_C = 8.0
MLA_HEADS = 8
MLA_NOPE = 64
MLA_ROPE = 32
MLA_V = 64
MLA_Q_RANK = 384
MLA_KV_RANK = 256
DIFF_HEADS = 4
DIFF_D = 64
DIFF_V = 128
GQA_Q_HEADS = 8
GQA_KV_HEADS = 2
GQA_GROUP = 4
WINDOW = 128
MACARON_WEIGHT = 0.5
ROPE_BASE = 10000.0
RMS_EPS = 1e-6
NEG_INF = -1e30
LOG2E = 1.4426950408889634

LANES = 128
SUBLANES = 8
MLA_PAD = 128

COLS_LRU = 2 * GROUP_W
COLS_MLA = MLA_Q_RANK + MLA_KV_RANK + MLA_ROPE
DIFF_QK = 2 * DIFF_HEADS * 2 * DIFF_D
COLS_DIFF = DIFF_QK + DIFF_HEADS * DIFF_V
GQA_QK = (GQA_Q_HEADS + GQA_KV_HEADS) * HEAD_DIM
GQA_KVW = GQA_KV_HEADS * HEAD_DIM
COLS_GQA = GQA_QK + GQA_KVW
P_LRU = 0
P_MLA = COLS_LRU
W_MLA = 768
P_DIFF = P_MLA + W_MLA
P_GQA = P_DIFF + DIFF_QK
N_IN_P = P_GQA + GQA_QK

VMEM_LIMIT = 60 * 1024 * 1024


def _cparams(sem):
    return pltpu.CompilerParams(dimension_semantics=sem, vmem_limit_bytes=VMEM_LIMIT)


def _rms(x, g):
    ms = jnp.mean(x * x, axis=-1, keepdims=True)
    return x * lax.rsqrt(ms + RMS_EPS) * g


def _dot(a, b):
    return jnp.dot(a, b, preferred_element_type=F32)


def _dot_nt(a, b):
    return lax.dot_general(a, b, (((1,), (1,)), ((), ())), preferred_element_type=F32)


def _layer_spec(arr, layer, *lead):
    shape = arr.shape[1 + len(lead):]
    return pl.BlockSpec((None,) * (1 + len(lead)) + shape,
                        lambda *_: (layer,) + lead + (0,) * len(shape),
                        pipeline_mode=pl.Buffered(1))


def _mod_spec(d, layer, row_fn, k):
    return pl.BlockSpec((None, None, 1, d), lambda i, *_: (layer, row_fn(i), 0, k))


def _skip_first(kernel):
    def wrapped(_, *refs, **kw):
        return kernel(*refs, **kw)
    return wrapped


def _mod_kernel(c_ref, w_ref, b_ref, o_ref):
    s = c_ref[...]
    s = s * jax.nn.sigmoid(s)
    o_ref[...] = _dot(s.astype(BF16), w_ref[...].astype(BF16)) + b_ref[...]


def _modulation(cvec, w_mod, b_mod, tn):
    depth, d, n = w_mod.shape
    rows = cvec.shape[0]
    return pl.pallas_call(
        _mod_kernel,
        grid=(depth, n // tn),
        in_specs=[
            pl.BlockSpec((rows, d), lambda l, j: (0, 0)),
            pl.BlockSpec((None, d, tn), lambda l, j: (l, 0, j)),
            pl.BlockSpec((None, 1, tn), lambda l, j: (l, 0, j)),
        ],
        out_specs=pl.BlockSpec((None, rows, tn), lambda l, j: (l, 0, j)),
        out_shape=jax.ShapeDtypeStruct((depth, rows, n), F32),
        compiler_params=_cparams(("arbitrary", "arbitrary")),
        name="modulation",
    )(cvec, w_mod, b_mod.reshape(depth, 1, n))


FFN_ROW_CHUNK = 64


def _ffn_kernel(x_ref, shift_ref, scale_ref, gate_ref, gpre_ref, gpost_ref,
                wg_ref, wu_ref, wo_ref, o_ref, h_ref, *, nj):
    j = pl.program_id(1)
    n_chunks = x_ref.shape[0] // FFN_ROW_CHUNK

    def rows(r):
        return pl.ds(pl.multiple_of(r * FFN_ROW_CHUNK, FFN_ROW_CHUNK), FFN_ROW_CHUNK)

    @pl.when(j == 0)
    def _():
        mul = gpre_ref[...] * (1.0 + scale_ref[...])
        shift = shift_ref[...]

        def body(r, carry):
            x = x_ref[rows(r), :]
            ms = jnp.mean(x * x, axis=-1, keepdims=True)
            h_ref[rows(r), :] = (x * lax.rsqrt(ms + RMS_EPS) * mul + shift).astype(BF16)
            return carry

        lax.fori_loop(0, n_chunks, body, 0, unroll=2)
        o_ref[...] = jnp.zeros_like(o_ref)

    h = h_ref[...]
    g = _dot(h, wg_ref[...].astype(BF16))
    u = _dot(h, wu_ref[...].astype(BF16))
    a = (g * jax.nn.sigmoid(g) * u).astype(BF16)
    o_ref[...] += _dot(a, wo_ref[...].astype(BF16))

    @pl.when(j == nj - 1)
    def _():
        mul = (MACARON_WEIGHT * gate_ref[...]) * gpost_ref[...]

        def body(r, carry):
            y = o_ref[rows(r), :]
            ms = jnp.mean(y * y, axis=-1, keepdims=True)
            o_ref[rows(r), :] = x_ref[rows(r), :] + y * lax.rsqrt(ms + RMS_EPS) * mul
            return carry

        lax.fori_loop(0, n_chunks, body, 0)


def _ffn_sublayer(x_in, n_blocks, out_rows, out_blk0, prev_out, mod4, k0, norm4, gi,
                  w_in_all, w_out_all, layer, sub, row_fn, tm, tf):
    d = x_in.shape[1]
    d_ff = w_out_all.shape[2]
    nj = d_ff // tf
    out_row = lambda i: row_fn(out_blk0 + i)
    g_spec = lambda idx: pl.BlockSpec((None, None, 1, d), lambda i, j: (layer, idx, 0, 0))
    lead_specs, lead_args, aliases, body = [], [], {}, _ffn_kernel
    if prev_out is not None:
        lead_specs, lead_args, aliases = [pl.BlockSpec(memory_space=pl.ANY)], [prev_out], {0: 0}
        body = _skip_first(_ffn_kernel)
    return pl.pallas_call(
        functools.partial(body, nj=nj),
        grid=(n_blocks, nj),
        in_specs=lead_specs + [
            pl.BlockSpec((tm, d), lambda i, j: (i, 0), pipeline_mode=pl.Buffered(1)),
            _mod_spec(d, layer, out_row, k0), _mod_spec(d, layer, out_row, k0 + 1),
            _mod_spec(d, layer, out_row, k0 + 2),
            g_spec(gi), g_spec(gi + 1),
            pl.BlockSpec((None, None, d, tf), lambda i, j: (layer, sub, 0, j)),
            pl.BlockSpec((None, None, d, tf), lambda i, j: (layer, sub, 0, nj + j)),
            pl.BlockSpec((None, None, tf, d), lambda i, j: (layer, sub, j, 0)),
        ],
        out_specs=pl.BlockSpec((tm, d), lambda i, j: (out_blk0 + i, 0), pipeline_mode=pl.Buffered(1)),
        out_shape=jax.ShapeDtypeStruct((out_rows, d), F32),
        scratch_shapes=[pltpu.VMEM((tm, d), BF16)],
        input_output_aliases=aliases,
        compiler_params=_cparams(("arbitrary", "arbitrary")),
        name="ffn_sublayer",
    )(*lead_args, x_in, mod4, mod4, mod4, norm4, norm4, w_in_all, w_in_all, w_out_all)


def _swap_halves(x, first, half):
    return jnp.where(first, pltpu.roll(x, LANES - half, 1), pltpu.roll(x, half, 1))


def _rope_chunks(x, cos, sin, first, half):
    out = []
    for c in range(x.shape[1] // LANES):
        blk = x[:, c * LANES:(c + 1) * LANES]
        out.append(blk * cos + _swap_halves(blk, first, half) * sin)
    return out[0] if len(out) == 1 else jnp.concatenate(out, axis=1)


def _mixer_in_kernel(x_ref, shift_ref, scale_ref, g_ref, w_ref, wvt_ref,
                     cos_hd_ref, sin_hd_ref, cos_mla_ref, sin_mla_ref,
                     qn_ref, wq_ref, kvn_ref, wk_ref, wmvt_ref,
                     lru_ref, mq_ref, mk_ref, mvt_ref, dq_ref, dk_ref, dvt_ref,
                     gq_ref, gk_ref, gvt_ref):
    h = (_rms(x_ref[...], g_ref[...]) * (1.0 + scale_ref[...]) + shift_ref[...]).astype(BF16)
    tm = h.shape[0]
    lane = lax.broadcasted_iota(jnp.int32, (tm, LANES), 1)
    first_hd = (lane % HEAD_DIM) < (HEAD_DIM // 2)
    first_mla = lane < (MLA_NOPE + MLA_ROPE // 2)
    cos_hd, sin_hd = cos_hd_ref[...], sin_hd_ref[...]
    cos_mla, sin_mla = cos_mla_ref[...], sin_mla_ref[...]

    lru_ref[...] = _dot(h, w_ref[:, P_LRU:P_LRU + COLS_LRU])

    cm = _dot(h, w_ref[:, P_MLA:P_MLA + W_MLA])
    qn = _rms(cm[:, :MLA_Q_RANK], qn_ref[...]).astype(BF16)
    q = _rope_chunks(_dot(qn, wq_ref[...]), cos_mla, sin_mla, first_mla, MLA_ROPE // 2)
    mq_ref[...] = (q * (LOG2E * (MLA_NOPE + MLA_ROPE) ** -0.5)).astype(BF16)
    kvn = _rms(cm[:, MLA_Q_RANK:MLA_Q_RANK + MLA_KV_RANK], kvn_ref[...]).astype(BF16)
    k_rope = _rope_chunks(cm[:, W_MLA - LANES:], cos_mla, sin_mla, first_mla, MLA_ROPE // 2)
    k_nope = _dot(kvn, wk_ref[...])
    mk_ref[...] = (k_nope + jnp.concatenate([k_rope] * MLA_HEADS, axis=1)).astype(BF16)
    mvt_ref[...] = _dot_nt(wmvt_ref[...], kvn).astype(BF16)

    cd = _dot(h, w_ref[:, P_DIFF:P_DIFF + DIFF_QK])
    half = DIFF_QK // 2
    dq_ref[...] = (_rope_chunks(cd[:, :half], cos_hd, sin_hd, first_hd, HEAD_DIM // 2)
                   * (LOG2E * DIFF_D ** -0.5)).astype(BF16)
    dk_ref[...] = _rope_chunks(cd[:, half:], cos_hd, sin_hd, first_hd, HEAD_DIM // 2).astype(BF16)

    nq = GQA_Q_HEADS * HEAD_DIM
    cg = _dot(h, w_ref[:, P_GQA:P_GQA + GQA_QK])
    gq_ref[...] = (_rope_chunks(cg[:, :nq], cos_hd, sin_hd, first_hd, HEAD_DIM // 2)
                   * (LOG2E * HEAD_DIM ** -0.5)).astype(BF16)
    gk_ref[...] = _rope_chunks(cg[:, nq:], cos_hd, sin_hd, first_hd, HEAD_DIM // 2).astype(BF16)

    vt = _dot_nt(wvt_ref[...], h)
    dvt_ref[...] = vt[:GROUP_W].astype(BF16)
    gvt_ref[...] = vt[GROUP_W:].astype(BF16)


def _mixer_in(xs, mod4, norm4, w_in_p, w_vt, tabs, mla_w, layer, row_fn, tab_fn, tm):
    n_rows, d = xs.shape
    cos_hd, sin_hd, cos_mla, sin_mla = tabs
    qn, wq, kvn, wk, wmvt = mla_w
    tab_spec = pl.BlockSpec((tm, LANES), lambda i: (tab_fn(i), 0))
    outs = [(COLS_LRU, F32, False),
            (MLA_HEADS * MLA_PAD, BF16, False), (MLA_HEADS * MLA_PAD, BF16, False),
            (GROUP_W, BF16, True),
            (GROUP_W, BF16, False), (GROUP_W, BF16, False), (GROUP_W, BF16, True),
            (GROUP_W, BF16, False), (GQA_KVW, BF16, False), (GQA_KVW, BF16, True)]
    out_specs = [pl.BlockSpec((w, tm), lambda i: (0, i)) if t else pl.BlockSpec((tm, w), lambda i: (i, 0))
                 for w, _, t in outs]
    out_shape = [jax.ShapeDtypeStruct((w, n_rows) if t else (n_rows, w), dt) for w, dt, t in outs]
    return pl.pallas_call(
        _mixer_in_kernel,
        grid=(n_rows // tm,),
        in_specs=[
            pl.BlockSpec((tm, d), lambda i: (i, 0)),
            _mod_spec(d, layer, row_fn, 3), _mod_spec(d, layer, row_fn, 4),
            _layer_spec(norm4, layer, 2), _layer_spec(w_in_p, layer), _layer_spec(w_vt, layer),
            tab_spec, tab_spec, tab_spec, tab_spec,
            _layer_spec(qn, layer), _layer_spec(wq, layer), _layer_spec(kvn, layer),
            _layer_spec(wk, layer), _layer_spec(wmvt, layer),
        ],
        out_specs=out_specs,
        out_shape=out_shape,
        compiler_params=_cparams(("arbitrary",)),
        name="mixer_in",
    )(xs, mod4, mod4, norm4, w_in_p, w_vt, cos_hd, sin_hd, cos_mla, sin_mla, qn, wq, kvn, wk, wmvt)


def _sigmoid(x):
    return 0.5 * jnp.tanh(0.5 * x) + 0.5


def _lru_direction(x, prev, nxt, cw, cb, wg, bg, lam, reverse, carry_ref, h_ref, a_scr, b_scr):
    tc, ch = x.shape
    ext = jnp.concatenate([prev, x, nxt], axis=0)
    n_ext = tc + 2 * SUBLANES

    def tap(shift):
        return pltpu.roll(ext, (-shift) % n_ext, 0)[SUBLANES:SUBLANES + tc]

    u = cw[0:1] * tap(-2) + cw[1:2] * tap(-1) + cw[2:3] * x + cw[3:4] * tap(1) + cb
    g = _dot(u.astype(BF16), wg) + bg
    r = _sigmoid(g[:, :ch])
    i = _sigmoid(g[:, ch:])
    log_a = -LRU_C * r * jax.nn.softplus(-lam)
    a = jnp.exp(log_a)
    b = jnp.sqrt((1.0 - a) * (1.0 + a)) * (i * u)

    nt = tc // SUBLANES
    a = a.reshape(nt, SUBLANES, ch)
    b = b.reshape(nt, SUBLANES, ch)
    row = lax.broadcasted_iota(jnp.int32, (nt, SUBLANES, ch), 1)
    for s in (1, 2, 4):
        ok = (row < SUBLANES - s) if reverse else (row >= s)
        shift = SUBLANES - s if reverse else s
        a_sh, b_sh = pltpu.roll(a, shift, 1), pltpu.roll(b, shift, 1)
        b = a * jnp.where(ok, b_sh, 0.0) + b
        a = a * jnp.where(ok, a_sh, 1.0)
    a_scr[...] = a.reshape(tc, ch)
    b_scr[...] = b.reshape(tc, ch)

    def body(k, carry):
        kk = nt - 1 - k if reverse else k
        sl = pl.ds(pl.multiple_of(kk * SUBLANES, SUBLANES), SUBLANES)
        ht = b_scr[sl, :] + a_scr[sl, :] * carry
        h_ref[sl, :] = ht
        edge = ht[0:1, :] if reverse else ht[SUBLANES - 1:SUBLANES, :]
        return jnp.broadcast_to(edge, (SUBLANES, ch))

    carry_ref[...] = lax.fori_loop(0, nt, body, carry_ref[...])


def _lru_kernel(xf_ref, pf_ref, nf_ref, xb_ref, pb_ref, nb_ref,
                cw_ref, cb_ref, wgf_ref, wgb_ref, bgf_ref, bgb_ref, lamf_ref, lamb_ref,
                hf_ref, hb_ref, cf_ref, cbk_ref, a_scr, b_scr, *, halo_fn):
    s = pl.program_id(1)

    @pl.when(s == 0)
    def _():
        cf_ref[...] = jnp.zeros_like(cf_ref)
        cbk_ref[...] = jnp.zeros_like(cbk_ref)

    pvf, nvf, pvb, nvb = halo_fn(s)
    cw, cb = cw_ref[...], cb_ref[...]
    _lru_direction(xf_ref[...], pf_ref[...] * pvf, nf_ref[...] * nvf, cw, cb,
                   wgf_ref[...], bgf_ref[...], lamf_ref[...], False, cf_ref, hf_ref, a_scr, b_scr)
    _lru_direction(xb_ref[...], pb_ref[...] * pvb, nb_ref[...] * nvb, cw, cb,
                   wgb_ref[...], bgb_ref[...], lamb_ref[...], True, cbk_ref, hb_ref, a_scr, b_scr)


def _lru_mixer(lru_cols, conv_w, conv_b3, wg, bg4, lam4, layer, batch, seq, ctx_len, tc):
    n_rows = lru_cols.shape[0]
    ch = GROUP_W
    nl, ncx = seq // tc, ctx_len // tc
    n_steps = nl + ncx
    r8 = tc // SUBLANES
    last8 = n_rows // SUBLANES - 1

    def chunk_f(b, s):
        return jnp.where(s < ncx, batch * nl + b * ncx + s, b * nl + s - ncx)

    def chunk_b(b, s):
        return jnp.where(s < ncx, batch * nl + b * ncx + ncx - 1 - s, b * nl + nl - 1 - (s - ncx))

    def seg_pos(s, reverse):
        in_ctx = s < ncx
        if reverse:
            idx = jnp.where(in_ctx, ncx - 1 - s, nl - 1 - (s - ncx))
        else:
            idx = jnp.where(in_ctx, s, s - ncx)
        return idx, jnp.where(in_ctx, ncx, nl)

    def halo_fn(s):
        out = []
        for reverse in (False, True):
            idx, n = seg_pos(s, reverse)
            out += [(idx > 0).astype(F32), (idx < n - 1).astype(F32)]
        return out

    def x_spec(fn):
        return pl.BlockSpec((tc, ch), lambda b, s: (fn(b, s), 0))

    def prev_spec(fn):
        return pl.BlockSpec((SUBLANES, ch), lambda b, s: (jnp.maximum(fn(b, s) * r8 - 1, 0), 0))

    def next_spec(fn):
        return pl.BlockSpec((SUBLANES, ch), lambda b, s: (jnp.minimum((fn(b, s) + 1) * r8, last8), 0))

    return pl.pallas_call(
        functools.partial(_lru_kernel, halo_fn=halo_fn),
        grid=(batch, n_steps),
        in_specs=[
            x_spec(chunk_f), prev_spec(chunk_f), next_spec(chunk_f),
            x_spec(chunk_b), prev_spec(chunk_b), next_spec(chunk_b),
            _layer_spec(conv_w, layer), _layer_spec(conv_b3, layer),
            _layer_spec(wg, layer, 0), _layer_spec(wg, layer, 1),
            _layer_spec(bg4, layer, 0), _layer_spec(bg4, layer, 1),
            _layer_spec(lam4, layer, 0), _layer_spec(lam4, layer, 1),
        ],
        out_specs=[x_spec(chunk_f), x_spec(chunk_b)],
        out_shape=[jax.ShapeDtypeStruct((n_rows, ch), F32)] * 2,
        scratch_shapes=[pltpu.VMEM((SUBLANES, ch), F32), pltpu.VMEM((SUBLANES, ch), F32),
                        pltpu.VMEM((tc, ch), F32), pltpu.VMEM((tc, ch), F32)],
        compiler_params=_cparams(("arbitrary", "arbitrary")),
        name="rglru",
    )(lru_cols, lru_cols, lru_cols, lru_cols, lru_cols, lru_cols,
      conv_w, conv_b3, wg, wg, bg4, bg4, lam4, lam4)


def _softmax_cols(scores, extra=None):
    m = scores[0].max(axis=0, keepdims=True)
    for s in scores[1:]:
        m = jnp.maximum(m, s.max(axis=0, keepdims=True))
    if extra is not None:
        m = jnp.maximum(m, extra)
    ps = [jnp.exp2(s - m) for s in scores]
    den = ps[0].sum(axis=0, keepdims=True)
    for p in ps[1:]:
        den = den + p.sum(axis=0, keepdims=True)
    if extra is not None:
        den = den + jnp.exp2(extra - m)
    return ps, den


def _two_block_queries(q2, width):
    lane = lax.broadcasted_iota(jnp.int32, q2.shape, 1)
    zero = jnp.zeros((), q2.dtype)
    return jnp.concatenate([jnp.where(lane < width, q2, zero), jnp.where(lane >= width, q2, zero)], axis=0)


def _paired_probs(q_ref, ks, width, n_iter, finish, s_bufs):
    def issue(i):
        sl = slice(2 * i * width, 2 * (i + 1) * width)
        qq = _two_block_queries(q_ref[:, sl], width)
        buf, row, m = s_bufs[i % 2], 0, None
        for k in ks:
            s = _dot_nt(k[:, sl], qq)
            buf[row:row + k.shape[0], :] = s
            row += k.shape[0]
            ms = s.max(axis=0, keepdims=True)
            m = ms if m is None else jnp.maximum(m, ms)
        return m

    m_next = issue(0)
    for i in range(n_iter):
        m = m_next
        if i + 1 < n_iter:
            m_next = issue(i + 1)
        p = jnp.exp2(s_bufs[i % 2][...] - m)
        finish(i, p, p.sum(axis=0, keepdims=True))


def _weighted_values(vts, rows, p):
    o, row = None, 0
    for vt in vts:
        t = _dot(vt[rows, :], p[row:row + vt.shape[1], :])
        o = t if o is None else o + t
        row += vt.shape[1]
    return o


def _mla_kernel(q_ref, *refs, n_seg):
    ks, vts = refs[:n_seg], refs[n_seg:2 * n_seg]
    o_ref, s_bufs = refs[2 * n_seg], refs[2 * n_seg + 1:]
    tq = q_ref.shape[0]
    outs = []

    def finish(hp, p, den):
        p = p.astype(BF16)
        for e in range(2):
            h = 2 * hp + e
            cols = slice(e * tq, (e + 1) * tq)
            o = _weighted_values(vts, slice(h * MLA_V, (h + 1) * MLA_V), p[:, cols])
            outs.append(o / den[:, cols])

    _paired_probs(q_ref, ks, MLA_PAD, MLA_HEADS // 2, finish, s_bufs)
    o_ref[...] = jnp.concatenate(outs, axis=0).T.astype(o_ref.dtype)


def _diff_kernel(q_ref, lv_ref, g_ref, *refs, n_seg, lam_init):
    ks, vts = refs[:n_seg], refs[n_seg:2 * n_seg]
    o_ref, s_bufs = refs[2 * n_seg], refs[2 * n_seg + 1:]
    lv = lv_ref[...]
    lam = (jnp.exp(jnp.sum(lv[0:1] * lv[1:2], axis=-1, keepdims=True))
           - jnp.exp(jnp.sum(lv[2:3] * lv[3:4], axis=-1, keepdims=True)) + lam_init)
    tq = q_ref.shape[0]
    outs = []

    def finish(h, p, den):
        c1 = 1.0 / den[:, :tq]
        c2 = lam / den[:, tq:]
        w = (p[:, :tq] * c1 - p[:, tq:] * c2).astype(BF16)
        outs.append(_weighted_values(vts, slice(h * DIFF_V, (h + 1) * DIFF_V), w))

    _paired_probs(q_ref, ks, DIFF_D, DIFF_HEADS, finish, s_bufs)
    o = jnp.concatenate(outs, axis=0).T
    g = g_ref[...]
    for h in range(DIFF_HEADS):
        sl = slice(h * DIFF_V, (h + 1) * DIFF_V)
        o_ref[:, sl] = (_rms(o[:, sl], g) * (1.0 - lam_init)).astype(o_ref.dtype)


def _into(buf):
    if buf is None:
        return [], [], {}
    return [pl.BlockSpec(memory_space=pl.ANY)], [buf], {0: 0}


def _full_attention(kernel, q, k, vt, extra_specs, extra, batch, seq, ctx_len, tq,
                    ctx_queries, into, out_rows):
    qw, kw, vw = q.shape[1], k.shape[1], vt.shape[0]
    ctx_blk0 = batch * seq // ctx_len

    ctx_k = pl.BlockSpec((ctx_len, kw), lambda b, i: (ctx_blk0 + b, 0))
    ctx_v = pl.BlockSpec((vw, ctx_len), lambda b, i: (0, ctx_blk0 + b))
    lat_k = pl.BlockSpec((seq, kw), lambda b, i: (b, 0))
    lat_v = pl.BlockSpec((vw, seq), lambda b, i: (0, b))
    if ctx_queries:
        nq = ctx_len // tq
        q_map = lambda b, i: (batch * seq // tq + b * nq + i, 0)
        kv_specs, kv_args, n_seg = [ctx_k, ctx_v], [k, vt], 1
    else:
        nq = seq // tq
        q_map = lambda b, i: (b * nq + i, 0)
        kv_specs, kv_args, n_seg = [lat_k, ctx_k, lat_v, ctx_v], [k, k, vt, vt], 2
    lead_specs, lead_args, aliases = _into(into)
    body = functools.partial(kernel if into is None else _skip_first(kernel), n_seg=n_seg)
    n_keys = ctx_len if ctx_queries else seq + ctx_len
    return pl.pallas_call(
        body,
        grid=(batch, nq),
        in_specs=lead_specs + [pl.BlockSpec((tq, qw), q_map)] + extra_specs + kv_specs,
        out_specs=pl.BlockSpec((tq, GROUP_W), q_map),
        out_shape=jax.ShapeDtypeStruct((out_rows, GROUP_W), BF16),
        scratch_shapes=[pltpu.VMEM((n_keys, 2 * tq), F32)] * 2,
        input_output_aliases=aliases,
        compiler_params=_cparams(("arbitrary", "arbitrary")),
        name="full_attention",
    )(*lead_args, q, *extra, *kv_args)


def _gqa_group(q_ref, kv, ks, vts, masks, sinks, outs):
    ksl = slice(kv * HEAD_DIM, (kv + 1) * HEAD_DIM)
    heads = range(kv * GQA_GROUP, (kv + 1) * GQA_GROUP)
    nq = q_ref.shape[0]
    q4 = jnp.concatenate([q_ref[:, h * HEAD_DIM:(h + 1) * HEAD_DIM] for h in heads], axis=0)
    scores = []
    for k, mask in zip(ks, masks):
        s = _dot_nt(k[:, ksl], q4)
        scores.append(s if mask is None else jnp.where(mask, s, NEG_INF))
    sink = jnp.concatenate([jnp.full((1, nq), sinks(h) * LOG2E, F32) for h in heads], axis=1)
    ps, den = _softmax_cols(scores, extra=sink)
    o = None
    for p, vt in zip(ps, vts):
        t = _dot(vt[ksl, :], p.astype(BF16))
        o = t if o is None else o + t
    o = o / den
    outs += [o[:, g * nq:(g + 1) * nq] for g in range(GQA_GROUP)]


def _gqa_lat_kernel(sink_ref, q_ref, kl_ref, vtl_ref, kc_ref, vtc_ref, o_ref, *, seq, layer):
    n = pl.program_id(1)
    tq = q_ref.shape[0]
    span = tq + 2 * WINDOW
    start = pl.multiple_of(jnp.clip(n * tq - WINDOW, 0, seq - span), WINDOW)
    kw = kl_ref[pl.ds(start, span), :]
    vtw = vtl_ref[:, pl.ds(start, span)]
    kpos = start + lax.broadcasted_iota(jnp.int32, (span, GQA_GROUP * tq), 0)
    qpos = n * tq + lax.broadcasted_iota(jnp.int32, (span, GQA_GROUP * tq), 1) % tq
    valid = jnp.abs(kpos - qpos) <= WINDOW
    sinks = lambda h: sink_ref[layer, h]
    outs = []
    for kv in range(GQA_KV_HEADS):
        _gqa_group(q_ref, kv, [kc_ref[...], kw], [vtc_ref[...], vtw], [None, valid], sinks, outs)
    o_ref[...] = jnp.concatenate(outs, axis=0).T.astype(o_ref.dtype)


def _gqa_ctx_kernel(sink_ref, q_ref, kc_ref, vtc_ref, o_ref, *, layer):
    sinks = lambda h: sink_ref[layer, h]
    outs = []
    for kv in range(GQA_KV_HEADS):
        _gqa_group(q_ref, kv, [kc_ref[...]], [vtc_ref[...]], [None], sinks, outs)
    o_ref[...] = jnp.concatenate(outs, axis=0).T.astype(o_ref.dtype)


def _gqa_attention(q, k, vt, sink, layer, batch, seq, ctx_len, tq, ctx_queries, into, out_rows):
    ctx_blk0 = batch * seq // ctx_len
    smem = pl.BlockSpec(memory_space=pltpu.SMEM)
    ctx_k = pl.BlockSpec((ctx_len, GQA_KVW), lambda b, i: (ctx_blk0 + b, 0))
    ctx_v = pl.BlockSpec((GQA_KVW, ctx_len), lambda b, i: (0, ctx_blk0 + b))
    if ctx_queries:
        nq = ctx_len // tq
        q_map = lambda b, i: (batch * seq // tq + b * nq + i, 0)
        kernel = functools.partial(_gqa_ctx_kernel, layer=layer)
        specs = [smem, pl.BlockSpec((tq, GROUP_W), q_map), ctx_k, ctx_v]
        args = [sink, q, k, vt]
    else:
        nq = seq // tq
        q_map = lambda b, i: (b * nq + i, 0)
        lat_k = pl.BlockSpec((seq, GQA_KVW), lambda b, i: (b, 0))
        lat_v = pl.BlockSpec((GQA_KVW, seq), lambda b, i: (0, b))
        kernel = functools.partial(_gqa_lat_kernel, seq=seq, layer=layer)
        specs = [smem, pl.BlockSpec((tq, GROUP_W), q_map), lat_k, lat_v, ctx_k, ctx_v]
        args = [sink, q, k, vt, k, vt]
    lead_specs, lead_args, aliases = _into(into)
    if into is not None:
        kernel = _skip_first(kernel)
    return pl.pallas_call(
        kernel,
        grid=(batch, nq),
        in_specs=lead_specs + specs,
        out_specs=pl.BlockSpec((tq, GROUP_W), q_map),
        out_shape=jax.ShapeDtypeStruct((out_rows, GROUP_W), BF16),
        input_output_aliases=aliases,
        compiler_params=_cparams(("arbitrary", "arbitrary")),
        name="gqa_attention",
    )(*lead_args, *args)


def _mixer_out_kernel(x_ref, gate_ref, g_ref, hf_ref, hb_ref, lg_ref, yb_ref, yc_ref, yd_ref,
                      w_ref, o_ref):
    ya = ((hf_ref[...] + hb_ref[...]) * jax.nn.gelu(lg_ref[...])).astype(BF16)
    w = lambda g: w_ref[g * GROUP_W:(g + 1) * GROUP_W, :].astype(BF16)
    y = _dot(ya, w(0))
    y += _dot(yb_ref[...], w(1))
    y += _dot(yc_ref[...], w(2))
    y += _dot(yd_ref[...], w(3))
    o_ref[...] = x_ref[...] + gate_ref[...] * _rms(y, g_ref[...])


def _mixer_out(xs, n_rows, mod4, norm4, hf, hb, lru_cols, yb, yc, yd, w_out, layer, row_fn, tm):
    d = xs.shape[1]
    row_spec = pl.BlockSpec((tm, GROUP_W), lambda i: (i, 0))
    return pl.pallas_call(
        _mixer_out_kernel,
        grid=(n_rows // tm,),
        in_specs=[
            pl.BlockSpec((tm, d), lambda i: (i, 0)),
            _mod_spec(d, layer, row_fn, 5), _layer_spec(norm4, layer, 3),
            row_spec, row_spec,
            pl.BlockSpec((tm, GROUP_W), lambda i: (i, 1)),
            row_spec, row_spec, row_spec,
            _layer_spec(w_out, layer),
        ],
        out_specs=pl.BlockSpec((tm, d), lambda i: (i, 0)),
        out_shape=jax.ShapeDtypeStruct((n_rows, d), F32),
        compiler_params=_cparams(("arbitrary",)),
        name="mixer_out",
    )(xs, mod4, norm4, hf, hb, lru_cols, yb, yc, yd, w_out)


def _pack_w_in(w_in):
    depth, d, _ = w_in.shape
    o_mla = COLS_LRU
    o_diff = o_mla + COLS_MLA
    o_gqa = o_diff + COLS_DIFF
    z = lambda n: jnp.zeros((depth, d, n), w_in.dtype)
    lat = MLA_Q_RANK + MLA_KV_RANK
    parts = [
        w_in[:, :, :COLS_LRU],
        w_in[:, :, o_mla:o_mla + lat], z(MLA_NOPE),
        w_in[:, :, o_mla + lat:o_diff], z(MLA_PAD - MLA_NOPE - MLA_ROPE),
        w_in[:, :, o_diff:o_diff + DIFF_QK],
        w_in[:, :, o_gqa:o_gqa + GQA_QK],
    ]
    vals = jnp.concatenate([w_in[:, :, o_diff + DIFF_QK:o_gqa], w_in[:, :, o_gqa + GQA_QK:]], axis=-1)
    return jnp.concatenate(parts, axis=-1).astype(BF16), jnp.swapaxes(vals, 1, 2).astype(BF16)


def _pack_mla(w_qb, w_kvb):
    depth = w_qb.shape[0]
    hd = MLA_NOPE + MLA_ROPE
    wq = w_qb.reshape(depth, MLA_Q_RANK, MLA_HEADS, hd)
    wq = jnp.pad(wq, ((0, 0), (0, 0), (0, 0), (0, MLA_PAD - hd)))
    wkv = w_kvb.reshape(depth, MLA_KV_RANK, MLA_HEADS, MLA_NOPE + MLA_V)
    wk = jnp.pad(wkv[..., :MLA_NOPE], ((0, 0), (0, 0), (0, 0), (0, MLA_PAD - MLA_NOPE)))
    wv = wkv[..., MLA_NOPE:].reshape(depth, MLA_KV_RANK, MLA_HEADS * MLA_V)
    return (wq.reshape(depth, MLA_Q_RANK, MLA_HEADS * MLA_PAD).astype(BF16),
            wk.reshape(depth, MLA_KV_RANK, MLA_HEADS * MLA_PAD).astype(BF16),
            jnp.swapaxes(wv, 1, 2).astype(BF16))


def _pack_gates(w_gates):
    depth = w_gates.shape[0]
    eye = jnp.eye(LRU_HEADS, dtype=w_gates.dtype)
    dense = jnp.einsum('ldghij,hk->ldghikj', w_gates, eye).reshape(depth, 2, 2, GROUP_W, GROUP_W)
    return jnp.concatenate([dense[:, :, 0], dense[:, :, 1]], axis=-1).astype(BF16)


def _rope_tables(rows, rot_dim, lane0, period, pad_rows):
    row = jnp.repeat(jnp.arange(rows, dtype=F32), GRID_W)
    col = jnp.tile(jnp.arange(GRID_W, dtype=F32), rows)
    per_axis = rot_dim // 2
    inv = ROPE_BASE ** (-jnp.arange(0, per_axis, 2, dtype=F32) / per_axis)
    ang = jnp.concatenate([row[:, None] * inv, col[:, None] * inv], axis=-1)
    cos, sin = jnp.cos(ang), jnp.sin(ang)
    lane = jnp.arange(LANES)
    rel = (lane - lane0) % period
    active = (lane >= lane0) & (rel < rot_dim)
    idx = rel % per_axis
    sign = jnp.where(rel < per_axis, -1.0, 1.0)
    cos_t = jnp.where(active[None, :], cos[:, idx], 1.0)
    sin_t = jnp.where(active[None, :], sin[:, idx] * sign[None, :], 0.0)
    cos_t = jnp.concatenate([cos_t, jnp.ones((pad_rows, LANES), F32)], axis=0)
    sin_t = jnp.concatenate([sin_t, jnp.zeros((pad_rows, LANES), F32)], axis=0)
    return cos_t, sin_t


def _pick(n, prefs):
    for p in prefs:
        if n % p == 0:
            return p
    return n


def kernel(x, c, ctx, c_ctx, w_mod, b_mod, norm_g, ffn_w_in, ffn_w_out, w_in, w_out,
           lru_conv_w, lru_conv_b, lru_w_gates, lru_b_gates, lru_lambda,
           mla_q_norm, mla_w_qb, mla_kv_norm, mla_w_kvb, diff_lambda, diff_norm, gqa_sink):
    batch, seq, d = x.shape
    ctx_len = ctx.shape[1]
    depth = w_mod.shape[0]
    d_ff = ffn_w_out.shape[2]
    n_lat, n_ctx = batch * seq, batch * ctx_len
    n_all = n_lat + n_ctx

    tm_ffn = _pick(math.gcd(seq, n_ctx), (1024, 512, 256))
    tf = _pick(d_ff, (512, 256, 128))
    tm_mix = _pick(math.gcd(seq, n_ctx), (512, 256, 128))
    tq = _pick(math.gcd(seq, ctx_len), (256, 128))
    tq_lat = _pick(seq, (512, 256, 128))
    tq_gqa = _pick(seq, (2 * WINDOW, WINDOW))
    tc = _pick(math.gcd(seq, ctx_len), (256, 128))
    tn_mod = _pick(N_MOD * d, (1024, 512, 256))
    assert seq % GRID_W == 0 and seq >= tq_gqa + 2 * WINDOW and ctx_len % WINDOW == 0

    def rows_fn(tm):
        per, nlb = seq // tm, n_lat // tm
        return lambda i: jnp.where(i < nlb, i // per, batch)

    def tab_fn(tm):
        per, nlb = seq // tm, n_lat // tm
        return lambda i: jnp.where(i < nlb, i % per, per)

    mod_rows = SUBLANES * ((batch + 1 + SUBLANES - 1) // SUBLANES)
    cvec = jnp.concatenate([c, c_ctx[None, :], jnp.zeros((mod_rows - batch - 1, d), F32)], axis=0)
    mod4 = _modulation(cvec, w_mod, b_mod, tn_mod).reshape(depth, mod_rows, 1, N_MOD * d)

    w_in_p, w_vt = _pack_w_in(w_in)
    wq_p, wk_p, wmvt_p = _pack_mla(mla_w_qb, mla_w_kvb)
    mla_w = (mla_q_norm[:, None, :], wq_p, mla_kv_norm[:, None, :], wk_p, wmvt_p)
    wg_p = _pack_gates(lru_w_gates)
    norm4 = norm_g[:, :, None, :]
    conv_b3 = lru_conv_b[:, None, :]
    bg4 = lru_b_gates.reshape(depth, 2, 1, 2 * GROUP_W)
    lam4 = lru_lambda[:, :, None, :]
    diff_g3 = diff_norm[:, None, :]
    rows = seq // GRID_W
    tabs = (_rope_tables(rows, HEAD_DIM, 0, HEAD_DIM, tm_mix)
            + _rope_tables(rows, MLA_ROPE, MLA_NOPE, LANES, tm_mix))

    nb_lat, nb_ctx = n_lat // tm_ffn, n_ctx // tm_ffn
    xs = None
    for layer in range(depth):
        last = layer == depth - 1
        lam_init = 0.8 - 0.6 * math.exp(-0.3 * layer)
        ffn = functools.partial(_ffn_sublayer, w_in_all=ffn_w_in, w_out_all=ffn_w_out, layer=layer,
                                row_fn=rows_fn(tm_ffn), tm=tm_ffn, tf=tf, mod4=mod4, norm4=norm4)

        if layer == 0:
            xs = ffn(x.reshape(n_lat, d), nb_lat, n_all, 0, jnp.zeros((n_all, d), F32), k0=0, gi=0, sub=0)
            xs = ffn(ctx.reshape(n_ctx, d), nb_ctx, n_all, nb_lat, xs, k0=0, gi=0, sub=0)
        else:
            xs = ffn(xs, nb_lat + nb_ctx, n_all, 0, None, k0=0, gi=0, sub=0)

        (lru_cols, mq, mk, mvt, dq, dk, dvt, gq, gk, gvt) = _mixer_in(
            xs, mod4, norm4, w_in_p, w_vt, tabs, mla_w, layer, rows_fn(tm_mix), tab_fn(tm_mix), tm_mix)

        hf, hb = _lru_mixer(lru_cols, lru_conv_w, conv_b3, wg_p, bg4, lam4, layer,
                            batch, seq, ctx_len, tc)

        diff_extra = [diff_lambda, diff_g3]
        diff_specs = [_layer_spec(diff_lambda, layer), _layer_spec(diff_g3, layer)]
        diff_k = functools.partial(_diff_kernel, lam_init=lam_init)
        n_out = n_lat if last else n_all
        fresh = lambda: None if last else jnp.zeros((n_all, GROUP_W), BF16)
        yb = _full_attention(_mla_kernel, mq, mk, mvt, [], [], batch, seq, ctx_len, tq_lat,
                             False, fresh(), n_out)
        yc = _full_attention(diff_k, dq, dk, dvt, diff_specs, diff_extra, batch, seq, ctx_len, tq_lat,
                             False, fresh(), n_out)
        yd = _gqa_attention(gq, gk, gvt, gqa_sink, layer, batch, seq, ctx_len, tq_gqa, False, fresh(), n_out)
        if not last:
            yb = _full_attention(_mla_kernel, mq, mk, mvt, [], [], batch, seq, ctx_len, tq, True, yb, n_out)
            yc = _full_attention(diff_k, dq, dk, dvt, diff_specs, diff_extra, batch, seq, ctx_len, tq,
                                 True, yc, n_out)
            yd = _gqa_attention(gq, gk, gvt, gqa_sink, layer, batch, seq, ctx_len, tq, True, yd, n_out)

        xs = _mixer_out(xs, n_out, mod4, norm4, hf, hb, lru_cols, yb, yc, yd, w_out, layer,
                        rows_fn(tm_mix), tm_mix)
        xs = ffn(xs, n_out // tm_ffn, n_out, 0, None, k0=6, gi=4, sub=1)

    return xs.reshape(batch, seq, d)
```

```python
import functools
import math

import jax
import jax.numpy as jnp
from jax import lax
from jax.experimental import pallas as pl
from jax.experimental.pallas import tpu as pltpu

F32 = jnp.float32
BF16 = jnp.bfloat16

GRID_W = 64
HEAD_DIM = 64
N_MOD = 9
GROUP_W = 512
LRU_HEADS = 8
LRU_BLOCK = 64
LRU_C = 8.0
MLA_HEADS = 8
MLA_NOPE = 64
MLA_ROPE = 32
MLA_V = 64
MLA_Q_RANK = 384
MLA_KV_RANK = 256
DIFF_HEADS = 4
DIFF_D = 64
DIFF_V = 128
GQA_Q_HEADS = 8
GQA_KV_HEADS = 2
GQA_GROUP = 4
WINDOW = 128
MACARON_WEIGHT = 0.5
ROPE_BASE = 10000.0
RMS_EPS = 1e-6
NEG_INF = -1e30
LOG2E = 1.4426950408889634

LANES = 128
SUBLANES = 8
MLA_PAD = 128

COLS_LRU = 2 * GROUP_W
COLS_MLA = MLA_Q_RANK + MLA_KV_RANK + MLA_ROPE
DIFF_QK = 2 * DIFF_HEADS * 2 * DIFF_D
COLS_DIFF = DIFF_QK + DIFF_HEADS * DIFF_V
GQA_QK = (GQA_Q_HEADS + GQA_KV_HEADS) * HEAD_DIM
GQA_KVW = GQA_KV_HEADS * HEAD_DIM
COLS_GQA = GQA_QK + GQA_KVW
P_LRU = 0
P_MLA = COLS_LRU
W_MLA = 768
P_DIFF = P_MLA + W_MLA
P_GQA = P_DIFF + DIFF_QK
N_IN_P = P_GQA + GQA_QK

VMEM_LIMIT = 60 * 1024 * 1024


def _cparams(sem):
    return pltpu.CompilerParams(dimension_semantics=sem, vmem_limit_bytes=VMEM_LIMIT)


def _rms(x, g):
    ms = jnp.mean(x * x, axis=-1, keepdims=True)
    return x * lax.rsqrt(ms + RMS_EPS) * g


def _dot(a, b):
    return jnp.dot(a, b, preferred_element_type=F32)


def _dot_nt(a, b):
    return lax.dot_general(a, b, (((1,), (1,)), ((), ())), preferred_element_type=F32)


def _layer_spec(arr, layer, *lead):
    shape = arr.shape[1 + len(lead):]
    return pl.BlockSpec((None,) * (1 + len(lead)) + shape,
                        lambda *_: (layer,) + lead + (0,) * len(shape),
                        pipeline_mode=pl.Buffered(1))


def _mod_spec(d, layer, row_fn, k):
    return pl.BlockSpec((None, None, 1, d), lambda i, *_: (layer, row_fn(i), 0, k))


def _skip_first(kernel):
    def wrapped(_, *refs, **kw):
        return kernel(*refs, **kw)
    return wrapped


def _mod_kernel(c_ref, w_ref, b_ref, o_ref):
    s = c_ref[...]
    s = s * jax.nn.sigmoid(s)
    o_ref[...] = _dot(s.astype(BF16), w_ref[...].astype(BF16)) + b_ref[...]


def _modulation(cvec, w_mod, b_mod, tn):
    depth, d, n = w_mod.shape
    rows = cvec.shape[0]
    return pl.pallas_call(
        _mod_kernel,
        grid=(depth, n // tn),
        in_specs=[
            pl.BlockSpec((rows, d), lambda l, j: (0, 0)),
            pl.BlockSpec((None, d, tn), lambda l, j: (l, 0, j)),
            pl.BlockSpec((None, 1, tn), lambda l, j: (l, 0, j)),
        ],
        out_specs=pl.BlockSpec((None, rows, tn), lambda l, j: (l, 0, j)),
        out_shape=jax.ShapeDtypeStruct((depth, rows, n), F32),
        compiler_params=_cparams(("arbitrary", "arbitrary")),
        name="modulation",
    )(cvec, w_mod, b_mod.reshape(depth, 1, n))


FFN_ROW_CHUNK = 64


def _ffn_kernel(x_ref, shift_ref, scale_ref, gate_ref, gpre_ref, gpost_ref,
                wg_ref, wu_ref, wo_ref, o_ref, h_ref, *, nj):
    j = pl.program_id(1)
    n_chunks = x_ref.shape[0] // FFN_ROW_CHUNK

    def rows(r):
        return pl.ds(pl.multiple_of(r * FFN_ROW_CHUNK, FFN_ROW_CHUNK), FFN_ROW_CHUNK)

    @pl.when(j == 0)
    def _():
        mul = gpre_ref[...] * (1.0 + scale_ref[...])
        shift = shift_ref[...]

        def body(r, carry):
            x = x_ref[rows(r), :]
            ms = jnp.mean(x * x, axis=-1, keepdims=True)
            h_ref[rows(r), :] = (x * lax.rsqrt(ms + RMS_EPS) * mul + shift).astype(BF16)
            return carry

        lax.fori_loop(0, n_chunks, body, 0, unroll=2)
        o_ref[...] = jnp.zeros_like(o_ref)

    h = h_ref[...]
    g = _dot(h, wg_ref[...].astype(BF16))
    u = _dot(h, wu_ref[...].astype(BF16))
    a = (g * jax.nn.sigmoid(g) * u).astype(BF16)
    o_ref[...] += _dot(a, wo_ref[...].astype(BF16))

    @pl.when(j == nj - 1)
    def _():
        mul = (MACARON_WEIGHT * gate_ref[...]) * gpost_ref[...]

        def inv_rms(r):
            y = o_ref[rows(r), :]
            return lax.rsqrt(jnp.mean(y * y, axis=-1, keepdims=True) + RMS_EPS)

        def body(r, inv):
            inv_next = inv_rms(jnp.minimum(r + 1, n_chunks - 1))
            o_ref[rows(r), :] = x_ref[rows(r), :] + o_ref[rows(r), :] * inv * mul
            return inv_next

        lax.fori_loop(0, n_chunks, body, inv_rms(0))


def _ffn_sublayer(x_in, n_blocks, out_rows, out_blk0, prev_out, mod4, k0, norm4, gi,
                  w_in_all, w_out_all, layer, sub, row_fn, tm, tf):
    d = x_in.shape[1]
    d_ff = w_out_all.shape[2]
    nj = d_ff // tf
    out_row = lambda i: row_fn(out_blk0 + i)
    g_spec = lambda idx: pl.BlockSpec((None, None, 1, d), lambda i, j: (layer, idx, 0, 0))
    lead_specs, lead_args, aliases, body = [], [], {}, _ffn_kernel
    if prev_out is not None:
        lead_specs, lead_args, aliases = [pl.BlockSpec(memory_space=pl.ANY)], [prev_out], {0: 0}
        body = _skip_first(_ffn_kernel)
    return pl.pallas_call(
        functools.partial(body, nj=nj),
        grid=(n_blocks, nj),
        in_specs=lead_specs + [
            pl.BlockSpec((tm, d), lambda i, j: (i, 0), pipeline_mode=pl.Buffered(1)),
            _mod_spec(d, layer, out_row, k0), _mod_spec(d, layer, out_row, k0 + 1),
            _mod_spec(d, layer, out_row, k0 + 2),
            g_spec(gi), g_spec(gi + 1),
            pl.BlockSpec((None, None, d, tf), lambda i, j: (layer, sub, 0, j)),
            pl.BlockSpec((None, None, d, tf), lambda i, j: (layer, sub, 0, nj + j)),
            pl.BlockSpec((None, None, tf, d), lambda i, j: (layer, sub, j, 0)),
        ],
        out_specs=pl.BlockSpec((tm, d), lambda i, j: (out_blk0 + i, 0), pipeline_mode=pl.Buffered(1)),
        out_shape=jax.ShapeDtypeStruct((out_rows, d), F32),
        scratch_shapes=[pltpu.VMEM((tm, d), BF16)],
        input_output_aliases=aliases,
        compiler_params=_cparams(("arbitrary", "arbitrary")),
        name="ffn_sublayer",
    )(*lead_args, x_in, mod4, mod4, mod4, norm4, norm4, w_in_all, w_in_all, w_out_all)


def _swap_halves(x, first, half):
    return jnp.where(first, pltpu.roll(x, LANES - half, 1), pltpu.roll(x, half, 1))


def _rope_chunks(x, cos, sin, first, half):
    out = []
    for c in range(x.shape[1] // LANES):
        blk = x[:, c * LANES:(c + 1) * LANES]
        out.append(blk * cos + _swap_halves(blk, first, half) * sin)
    return out[0] if len(out) == 1 else jnp.concatenate(out, axis=1)


def _mixer_in_kernel(x_ref, shift_ref, scale_ref, g_ref, w_ref, wvt_ref,
                     cos_hd_ref, sin_hd_ref, cos_mla_ref, sin_mla_ref,
                     qn_ref, wq_ref, kvn_ref, wk_ref, wmvt_ref,
                     lru_ref, mq_ref, mk_ref, mvt_ref, dq_ref, dk_ref, dvt_ref,
                     gq_ref, gk_ref, gvt_ref):
    h = (_rms(x_ref[...], g_ref[...]) * (1.0 + scale_ref[...]) + shift_ref[...]).astype(BF16)
    tm = h.shape[0]
    lane = lax.broadcasted_iota(jnp.int32, (tm, LANES), 1)
    first_hd = (lane % HEAD_DIM) < (HEAD_DIM // 2)
    first_mla = lane < (MLA_NOPE + MLA_ROPE // 2)
    cos_hd, sin_hd = cos_hd_ref[...], sin_hd_ref[...]
    cos_mla, sin_mla = cos_mla_ref[...], sin_mla_ref[...]

    lru_ref[...] = _dot(h, w_ref[:, P_LRU:P_LRU + COLS_LRU])

    cm = _dot(h, w_ref[:, P_MLA:P_MLA + W_MLA])
    qn = _rms(cm[:, :MLA_Q_RANK], qn_ref[...]).astype(BF16)
    q = _rope_chunks(_dot(qn, wq_ref[...]), cos_mla, sin_mla, first_mla, MLA_ROPE // 2)
    mq_ref[...] = (q * (LOG2E * (MLA_NOPE + MLA_ROPE) ** -0.5)).astype(BF16)
    kvn = _rms(cm[:, MLA_Q_RANK:MLA_Q_RANK + MLA_KV_RANK], kvn_ref[...]).astype(BF16)
    k_rope = _rope_chunks(cm[:, W_MLA - LANES:], cos_mla, sin_mla, first_mla, MLA_ROPE // 2)
    k_nope = _dot(kvn, wk_ref[...])
    mk_ref[...] = (k_nope + jnp.concatenate([k_rope] * MLA_HEADS, axis=1)).astype(BF16)
    mvt_ref[...] = _dot_nt(wmvt_ref[...], kvn).astype(BF16)

    cd = _dot(h, w_ref[:, P_DIFF:P_DIFF + DIFF_QK])
    half = DIFF_QK // 2
    dq_ref[...] = (_rope_chunks(cd[:, :half], cos_hd, sin_hd, first_hd, HEAD_DIM // 2)
                   * (LOG2E * DIFF_D ** -0.5)).astype(BF16)
    dk_ref[...] = _rope_chunks(cd[:, half:], cos_hd, sin_hd, first_hd, HEAD_DIM // 2).astype(BF16)

    nq = GQA_Q_HEADS * HEAD_DIM
    cg = _dot(h, w_ref[:, P_GQA:P_GQA + GQA_QK])
    gq_ref[...] = (_rope_chunks(cg[:, :nq], cos_hd, sin_hd, first_hd, HEAD_DIM // 2)
                   * (LOG2E * HEAD_DIM ** -0.5)).astype(BF16)
    gk_ref[...] = _rope_chunks(cg[:, nq:], cos_hd, sin_hd, first_hd, HEAD_DIM // 2).astype(BF16)

    vt = _dot_nt(wvt_ref[...], h)
    dvt_ref[...] = vt[:GROUP_W].astype(BF16)
    gvt_ref[...] = vt[GROUP_W:].astype(BF16)


def _mixer_in(xs, mod4, norm4, w_in_p, w_vt, tabs, mla_w, layer, row_fn, tab_fn, tm):
    n_rows, d = xs.shape
    cos_hd, sin_hd, cos_mla, sin_mla = tabs
    qn, wq, kvn, wk, wmvt = mla_w
    tab_spec = pl.BlockSpec((tm, LANES), lambda i: (tab_fn(i), 0))
    outs = [(COLS_LRU, F32, False),
            (MLA_HEADS * MLA_PAD, BF16, False), (MLA_HEADS * MLA_PAD, BF16, False),
            (GROUP_W, BF16, True),
            (GROUP_W, BF16, False), (GROUP_W, BF16, False), (GROUP_W, BF16, True),
            (GROUP_W, BF16, False), (GQA_KVW, BF16, False), (GQA_KVW, BF16, True)]
    out_specs = [pl.BlockSpec((w, tm), lambda i: (0, i)) if t else pl.BlockSpec((tm, w), lambda i: (i, 0))
                 for w, _, t in outs]
    out_shape = [jax.ShapeDtypeStruct((w, n_rows) if t else (n_rows, w), dt) for w, dt, t in outs]
    return pl.pallas_call(
        _mixer_in_kernel,
        grid=(n_rows // tm,),
        in_specs=[
            pl.BlockSpec((tm, d), lambda i: (i, 0)),
            _mod_spec(d, layer, row_fn, 3), _mod_spec(d, layer, row_fn, 4),
            _layer_spec(norm4, layer, 2), _layer_spec(w_in_p, layer), _layer_spec(w_vt, layer),
            tab_spec, tab_spec, tab_spec, tab_spec,
            _layer_spec(qn, layer), _layer_spec(wq, layer), _layer_spec(kvn, layer),
            _layer_spec(wk, layer), _layer_spec(wmvt, layer),
        ],
        out_specs=out_specs,
        out_shape=out_shape,
        compiler_params=_cparams(("arbitrary",)),
        name="mixer_in",
    )(xs, mod4, mod4, norm4, w_in_p, w_vt, cos_hd, sin_hd, cos_mla, sin_mla, qn, wq, kvn, wk, wmvt)


def _sigmoid(x):
    return 0.5 * jnp.tanh(0.5 * x) + 0.5


def _lru_direction(x, prev, nxt, cw, cb, wg, bg, lam, reverse, carry_ref, h_ref, a_scr, b_scr):
    tc, ch = x.shape
    ext = jnp.concatenate([prev, x, nxt], axis=0)
    n_ext = tc + 2 * SUBLANES

    def tap(shift):
        return pltpu.roll(ext, (-shift) % n_ext, 0)[SUBLANES:SUBLANES + tc]

    u = cw[0:1] * tap(-2) + cw[1:2] * tap(-1) + cw[2:3] * x + cw[3:4] * tap(1) + cb
    g = _dot(u.astype(BF16), wg) + bg
    r = _sigmoid(g[:, :ch])
    i = _sigmoid(g[:, ch:])
    log_a = -LRU_C * r * jax.nn.softplus(-lam)
    a = jnp.exp(log_a)
    b = jnp.sqrt((1.0 - a) * (1.0 + a)) * (i * u)

    nt = tc // SUBLANES
    a = a.reshape(nt, SUBLANES, ch)
    b = b.reshape(nt, SUBLANES, ch)
    row = lax.broadcasted_iota(jnp.int32, (nt, SUBLANES, ch), 1)
    for s in (1, 2, 4):
        ok = (row < SUBLANES - s) if reverse else (row >= s)
        shift = SUBLANES - s if reverse else s
        a_sh, b_sh = pltpu.roll(a, shift, 1), pltpu.roll(b, shift, 1)
        b = a * jnp.where(ok, b_sh, 0.0) + b
        a = a * jnp.where(ok, a_sh, 1.0)
    a_scr[...] = a.reshape(tc, ch)
    b_scr[...] = b.reshape(tc, ch)

    def body(k, carry):
        kk = nt - 1 - k if reverse else k
        sl = pl.ds(pl.multiple_of(kk * SUBLANES, SUBLANES), SUBLANES)
        ht = b_scr[sl, :] + a_scr[sl, :] * carry
        h_ref[sl, :] = ht
        edge = ht[0:1, :] if reverse else ht[SUBLANES - 1:SUBLANES, :]
        return jnp.broadcast_to(edge, (SUBLANES, ch))

    carry_ref[...] = lax.fori_loop(0, nt, body, carry_ref[...])


def _lru_kernel(xf_ref, pf_ref, nf_ref, xb_ref, pb_ref, nb_ref,
                cw_ref, cb_ref, wgf_ref, wgb_ref, bgf_ref, bgb_ref, lamf_ref, lamb_ref,
                hf_ref, hb_ref, cf_ref, cbk_ref, a_scr, b_scr, *, halo_fn):
    s = pl.program_id(1)

    @pl.when(s == 0)
    def _():
        cf_ref[...] = jnp.zeros_like(cf_ref)
        cbk_ref[...] = jnp.zeros_like(cbk_ref)

    pvf, nvf, pvb, nvb = halo_fn(s)
    cw, cb = cw_ref[...], cb_ref[...]
    _lru_direction(xf_ref[...], pf_ref[...] * pvf, nf_ref[...] * nvf, cw, cb,
                   wgf_ref[...], bgf_ref[...], lamf_ref[...], False, cf_ref, hf_ref, a_scr, b_scr)
    _lru_direction(xb_ref[...], pb_ref[...] * pvb, nb_ref[...] * nvb, cw, cb,
                   wgb_ref[...], bgb_ref[...], lamb_ref[...], True, cbk_ref, hb_ref, a_scr, b_scr)


def _lru_mixer(lru_cols, conv_w, conv_b3, wg, bg4, lam4, layer, batch, seq, ctx_len, tc):
    n_rows = lru_cols.shape[0]
    ch = GROUP_W
    nl, ncx = seq // tc, ctx_len // tc
    n_steps = nl + ncx
    r8 = tc // SUBLANES
    last8 = n_rows // SUBLANES - 1

    def chunk_f(b, s):
        return jnp.where(s < ncx, batch * nl + b * ncx + s, b * nl + s - ncx)

    def chunk_b(b, s):
        return jnp.where(s < ncx, batch * nl + b * ncx + ncx - 1 - s, b * nl + nl - 1 - (s - ncx))

    def seg_pos(s, reverse):
        in_ctx = s < ncx
        if reverse:
            idx = jnp.where(in_ctx, ncx - 1 - s, nl - 1 - (s - ncx))
        else:
            idx = jnp.where(in_ctx, s, s - ncx)
        return idx, jnp.where(in_ctx, ncx, nl)

    def halo_fn(s):
        out = []
        for reverse in (False, True):
            idx, n = seg_pos(s, reverse)
            out += [(idx > 0).astype(F32), (idx < n - 1).astype(F32)]
        return out

    def x_spec(fn):
        return pl.BlockSpec((tc, ch), lambda b, s: (fn(b, s), 0))

    def prev_spec(fn):
        return pl.BlockSpec((SUBLANES, ch), lambda b, s: (jnp.maximum(fn(b, s) * r8 - 1, 0), 0))

    def next_spec(fn):
        return pl.BlockSpec((SUBLANES, ch), lambda b, s: (jnp.minimum((fn(b, s) + 1) * r8, last8), 0))

    return pl.pallas_call(
        functools.partial(_lru_kernel, halo_fn=halo_fn),
        grid=(batch, n_steps),
        in_specs=[
            x_spec(chunk_f), prev_spec(chunk_f), next_spec(chunk_f),
            x_spec(chunk_b), prev_spec(chunk_b), next_spec(chunk_b),
            _layer_spec(conv_w, layer), _layer_spec(conv_b3, layer),
            _layer_spec(wg, layer, 0), _layer_spec(wg, layer, 1),
            _layer_spec(bg4, layer, 0), _layer_spec(bg4, layer, 1),
            _layer_spec(lam4, layer, 0), _layer_spec(lam4, layer, 1),
        ],
        out_specs=[x_spec(chunk_f), x_spec(chunk_b)],
        out_shape=[jax.ShapeDtypeStruct((n_rows, ch), F32)] * 2,
        scratch_shapes=[pltpu.VMEM((SUBLANES, ch), F32), pltpu.VMEM((SUBLANES, ch), F32),
                        pltpu.VMEM((tc, ch), F32), pltpu.VMEM((tc, ch), F32)],
        compiler_params=_cparams(("arbitrary", "arbitrary")),
        name="rglru",
    )(lru_cols, lru_cols, lru_cols, lru_cols, lru_cols, lru_cols,
      conv_w, conv_b3, wg, wg, bg4, bg4, lam4, lam4)


def _softmax_cols(scores, extra=None):
    m = scores[0].max(axis=0, keepdims=True)
    for s in scores[1:]:
        m = jnp.maximum(m, s.max(axis=0, keepdims=True))
    if extra is not None:
        m = jnp.maximum(m, extra)
    ps = [jnp.exp2(s - m) for s in scores]
    den = ps[0].sum(axis=0, keepdims=True)
    for p in ps[1:]:
        den = den + p.sum(axis=0, keepdims=True)
    if extra is not None:
        den = den + jnp.exp2(extra - m)
    return ps, den


def _two_block_queries(q2, width):
    lane = lax.broadcasted_iota(jnp.int32, q2.shape, 1)
    zero = jnp.zeros((), q2.dtype)
    return jnp.concatenate([jnp.where(lane < width, q2, zero), jnp.where(lane >= width, q2, zero)], axis=0)


def _paired_probs(q_ref, ks, width, n_iter, finish, s_bufs):
    def issue(i):
        sl = slice(2 * i * width, 2 * (i + 1) * width)
        qq = _two_block_queries(q_ref[:, sl], width)
        buf, row, m = s_bufs[i % 2], 0, None
        for k in ks:
            s = _dot_nt(k[:, sl], qq)
            buf[row:row + k.shape[0], :] = s
            row += k.shape[0]
            ms = s.max(axis=0, keepdims=True)
            m = ms if m is None else jnp.maximum(m, ms)
        return m

    m_next = issue(0)
    for i in range(n_iter):
        m = m_next
        if i + 1 < n_iter:
            m_next = issue(i + 1)
        p = jnp.exp2(s_bufs[i % 2][...] - m)
        finish(i, p, p.sum(axis=0, keepdims=True))


def _weighted_values(vts, rows, p):
    o, row = None, 0
    for vt in vts:
        t = _dot(vt[rows, :], p[row:row + vt.shape[1], :])
        o = t if o is None else o + t
        row += vt.shape[1]
    return o


def _mla_kernel(q_ref, *refs, n_seg):
    ks, vts = refs[:n_seg], refs[n_seg:2 * n_seg]
    o_ref, s_bufs = refs[2 * n_seg], refs[2 * n_seg + 1:]
    tq = q_ref.shape[0]
    outs = []

    def finish(hp, p, den):
        p = p.astype(BF16)
        for e in range(2):
            h = 2 * hp + e
            cols = slice(e * tq, (e + 1) * tq)
            o = _weighted_values(vts, slice(h * MLA_V, (h + 1) * MLA_V), p[:, cols])
            outs.append(o / den[:, cols])

    _paired_probs(q_ref, ks, MLA_PAD, MLA_HEADS // 2, finish, s_bufs)
    o_ref[...] = jnp.concatenate(outs, axis=0).T.astype(o_ref.dtype)


def _diff_kernel(q_ref, lv_ref, g_ref, *refs, n_seg, lam_init):
    ks, vts = refs[:n_seg], refs[n_seg:2 * n_seg]
    o_ref, s_bufs = refs[2 * n_seg], refs[2 * n_seg + 1:]
    lv = lv_ref[...]
    lam = (jnp.exp(jnp.sum(lv[0:1] * lv[1:2], axis=-1, keepdims=True))
           - jnp.exp(jnp.sum(lv[2:3] * lv[3:4], axis=-1, keepdims=True)) + lam_init)
    tq = q_ref.shape[0]
    outs = []

    def finish(h, p, den):
        c1 = 1.0 / den[:, :tq]
        c2 = lam / den[:, tq:]
        w = (p[:, :tq] * c1 - p[:, tq:] * c2).astype(BF16)
        outs.append(_weighted_values(vts, slice(h * DIFF_V, (h + 1) * DIFF_V), w))

    _paired_probs(q_ref, ks, DIFF_D, DIFF_HEADS, finish, s_bufs)
    o = jnp.concatenate(outs, axis=0).T
    g = g_ref[...]
    for h in range(DIFF_HEADS):
        sl = slice(h * DIFF_V, (h + 1) * DIFF_V)
        o_ref[:, sl] = (_rms(o[:, sl], g) * (1.0 - lam_init)).astype(o_ref.dtype)


def _into(buf):
    if buf is None:
        return [], [], {}
    return [pl.BlockSpec(memory_space=pl.ANY)], [buf], {0: 0}


def _full_attention(kernel, q, k, vt, extra_specs, extra, batch, seq, ctx_len, tq,
                    ctx_queries, into, out_rows):
    qw, kw, vw = q.shape[1], k.shape[1], vt.shape[0]
    ctx_blk0 = batch * seq // ctx_len

    ctx_k = pl.BlockSpec((ctx_len, kw), lambda b, i: (ctx_blk0 + b, 0))
    ctx_v = pl.BlockSpec((vw, ctx_len), lambda b, i: (0, ctx_blk0 + b))
    lat_k = pl.BlockSpec((seq, kw), lambda b, i: (b, 0))
    lat_v = pl.BlockSpec((vw, seq), lambda b, i: (0, b))
    if ctx_queries:
        nq = ctx_len // tq
        q_map = lambda b, i: (batch * seq // tq + b * nq + i, 0)
        kv_specs, kv_args, n_seg = [ctx_k, ctx_v], [k, vt], 1
    else:
        nq = seq // tq
        q_map = lambda b, i: (b * nq + i, 0)
        kv_specs, kv_args, n_seg = [lat_k, ctx_k, lat_v, ctx_v], [k, k, vt, vt], 2
    lead_specs, lead_args, aliases = _into(into)
    body = functools.partial(kernel if into is None else _skip_first(kernel), n_seg=n_seg)
    n_keys = ctx_len if ctx_queries else seq + ctx_len
    return pl.pallas_call(
        body,
        grid=(batch, nq),
        in_specs=lead_specs + [pl.BlockSpec((tq, qw), q_map)] + extra_specs + kv_specs,
        out_specs=pl.BlockSpec((tq, GROUP_W), q_map),
        out_shape=jax.ShapeDtypeStruct((out_rows, GROUP_W), BF16),
        scratch_shapes=[pltpu.VMEM((n_keys, 2 * tq), F32)] * 2,
        input_output_aliases=aliases,
        compiler_params=_cparams(("arbitrary", "arbitrary")),
        name="full_attention",
    )(*lead_args, q, *extra, *kv_args)


def _gqa_group(q_ref, kv, ks, vts, masks, sinks, outs):
    ksl = slice(kv * HEAD_DIM, (kv + 1) * HEAD_DIM)
    heads = range(kv * GQA_GROUP, (kv + 1) * GQA_GROUP)
    nq = q_ref.shape[0]
    q4 = jnp.concatenate([q_ref[:, h * HEAD_DIM:(h + 1) * HEAD_DIM] for h in heads], axis=0)
    scores = []
    for k, mask in zip(ks, masks):
        s = _dot_nt(k[:, ksl], q4)
        scores.append(s if mask is None else jnp.where(mask, s, NEG_INF))
    sink = jnp.concatenate([jnp.full((1, nq), sinks(h) * LOG2E, F32) for h in heads], axis=1)
    ps, den = _softmax_cols(scores, extra=sink)
    o = None
    for p, vt in zip(ps, vts):
        t = _dot(vt[ksl, :], p.astype(BF16))
        o = t if o is None else o + t
    o = o / den
    outs += [o[:, g * nq:(g + 1) * nq] for g in range(GQA_GROUP)]


def _gqa_lat_kernel(sink_ref, q_ref, kl_ref, vtl_ref, kc_ref, vtc_ref, o_ref, *, seq, layer):
    n = pl.program_id(1)
    tq = q_ref.shape[0]
    span = tq + 2 * WINDOW
    start = pl.multiple_of(jnp.clip(n * tq - WINDOW, 0, seq - span), WINDOW)
    kw = kl_ref[pl.ds(start, span), :]
    vtw = vtl_ref[:, pl.ds(start, span)]
    kpos = start + lax.broadcasted_iota(jnp.int32, (span, GQA_GROUP * tq), 0)
    qpos = n * tq + lax.broadcasted_iota(jnp.int32, (span, GQA_GROUP * tq), 1) % tq
    valid = jnp.abs(kpos - qpos) <= WINDOW
    sinks = lambda h: sink_ref[layer, h]
    outs = []
    for kv in range(GQA_KV_HEADS):
        _gqa_group(q_ref, kv, [kc_ref[...], kw], [vtc_ref[...], vtw], [None, valid], sinks, outs)
    o_ref[...] = jnp.concatenate(outs, axis=0).T.astype(o_ref.dtype)


def _gqa_ctx_kernel(sink_ref, q_ref, kc_ref, vtc_ref, o_ref, *, layer):
    sinks = lambda h: sink_ref[layer, h]
    outs = []
    for kv in range(GQA_KV_HEADS):
        _gqa_group(q_ref, kv, [kc_ref[...]], [vtc_ref[...]], [None], sinks, outs)
    o_ref[...] = jnp.concatenate(outs, axis=0).T.astype(o_ref.dtype)


def _gqa_attention(q, k, vt, sink, layer, batch, seq, ctx_len, tq, ctx_queries, into, out_rows):
    ctx_blk0 = batch * seq // ctx_len
    smem = pl.BlockSpec(memory_space=pltpu.SMEM)
    ctx_k = pl.BlockSpec((ctx_len, GQA_KVW), lambda b, i: (ctx_blk0 + b, 0))
    ctx_v = pl.BlockSpec((GQA_KVW, ctx_len), lambda b, i: (0, ctx_blk0 + b))
    if ctx_queries:
        nq = ctx_len // tq
        q_map = lambda b, i: (batch * seq // tq + b * nq + i, 0)
        kernel = functools.partial(_gqa_ctx_kernel, layer=layer)
        specs = [smem, pl.BlockSpec((tq, GROUP_W), q_map), ctx_k, ctx_v]
        args = [sink, q, k, vt]
    else:
        nq = seq // tq
        q_map = lambda b, i: (b * nq + i, 0)
        lat_k = pl.BlockSpec((seq, GQA_KVW), lambda b, i: (b, 0))
        lat_v = pl.BlockSpec((GQA_KVW, seq), lambda b, i: (0, b))
        kernel = functools.partial(_gqa_lat_kernel, seq=seq, layer=layer)
        specs = [smem, pl.BlockSpec((tq, GROUP_W), q_map), lat_k, lat_v, ctx_k, ctx_v]
        args = [sink, q, k, vt, k, vt]
    lead_specs, lead_args, aliases = _into(into)
    if into is not None:
        kernel = _skip_first(kernel)
    return pl.pallas_call(
        kernel,
        grid=(batch, nq),
        in_specs=lead_specs + specs,
        out_specs=pl.BlockSpec((tq, GROUP_W), q_map),
        out_shape=jax.ShapeDtypeStruct((out_rows, GROUP_W), BF16),
        input_output_aliases=aliases,
        compiler_params=_cparams(("arbitrary", "arbitrary")),
        name="gqa_attention",
    )(*lead_args, *args)


def _mixer_out_kernel(x_ref, gate_ref, g_ref, hf_ref, hb_ref, lg_ref, yb_ref, yc_ref, yd_ref,
                      w_ref, o_ref):
    ya = ((hf_ref[...] + hb_ref[...]) * jax.nn.gelu(lg_ref[...])).astype(BF16)
    w = lambda g: w_ref[g * GROUP_W:(g + 1) * GROUP_W, :].astype(BF16)
    y = _dot(ya, w(0))
    y += _dot(yb_ref[...], w(1))
    y += _dot(yc_ref[...], w(2))
    y += _dot(yd_ref[...], w(3))
    o_ref[...] = x_ref[...] + gate_ref[...] * _rms(y, g_ref[...])


def _mixer_out(xs, n_rows, mod4, norm4, hf, hb, lru_cols, yb, yc, yd, w_out, layer, row_fn, tm):
    d = xs.shape[1]
    row_spec = pl.BlockSpec((tm, GROUP_W), lambda i: (i, 0))
    return pl.pallas_call(
        _mixer_out_kernel,
        grid=(n_rows // tm,),
        in_specs=[
            pl.BlockSpec((tm, d), lambda i: (i, 0)),
            _mod_spec(d, layer, row_fn, 5), _layer_spec(norm4, layer, 3),
            row_spec, row_spec,
            pl.BlockSpec((tm, GROUP_W), lambda i: (i, 1)),
            row_spec, row_spec, row_spec,
            _layer_spec(w_out, layer),
        ],
        out_specs=pl.BlockSpec((tm, d), lambda i: (i, 0)),
        out_shape=jax.ShapeDtypeStruct((n_rows, d), F32),
        compiler_params=_cparams(("arbitrary",)),
        name="mixer_out",
    )(xs, mod4, norm4, hf, hb, lru_cols, yb, yc, yd, w_out)


def _pack_w_in_kernel(w_ref, p_ref, vt_ref):
    o_mla = COLS_LRU
    o_diff = o_mla + COLS_MLA
    o_gqa = o_diff + COLS_DIFF
    lat = MLA_Q_RANK + MLA_KV_RANK
    rows = w_ref.shape[0]
    z = lambda n: jnp.zeros((rows, n), BF16)
    cols = lambda a, b: w_ref[:, a:b].astype(BF16)
    p_ref[:, P_LRU:P_MLA + lat] = cols(0, o_mla + lat)
    p_ref[:, P_MLA + lat:P_DIFF] = jnp.concatenate(
        [z(MLA_NOPE), cols(o_mla + lat, o_diff), z(MLA_PAD - MLA_NOPE - MLA_ROPE)], axis=1)
    p_ref[:, P_DIFF:P_GQA] = cols(o_diff, o_diff + DIFF_QK)
    p_ref[:, P_GQA:N_IN_P] = cols(o_gqa, o_gqa + GQA_QK)
    vals = jnp.concatenate([w_ref[:, o_diff + DIFF_QK:o_gqa], w_ref[:, o_gqa + GQA_QK:]], axis=1)
    vt_ref[...] = vals.T.astype(BF16)


def _pack_w_in(w_in, tr):
    depth, d, n_in = w_in.shape
    nv = GROUP_W + GQA_KVW
    return pl.pallas_call(
        _pack_w_in_kernel,
        grid=(depth, d // tr),
        in_specs=[pl.BlockSpec((None, tr, n_in), lambda l, i: (l, i, 0))],
        out_specs=[pl.BlockSpec((None, tr, N_IN_P), lambda l, i: (l, i, 0)),
                   pl.BlockSpec((None, nv, tr), lambda l, i: (l, 0, i))],
        out_shape=[jax.ShapeDtypeStruct((depth, d, N_IN_P), BF16),
                   jax.ShapeDtypeStruct((depth, nv, d), BF16)],
        compiler_params=_cparams(("arbitrary", "arbitrary")),
        name="pack_w_in",
    )(w_in)


def _pack_mla(w_qb, w_kvb):
    depth = w_qb.shape[0]
    hd = MLA_NOPE + MLA_ROPE
    wq = w_qb.reshape(depth, MLA_Q_RANK, MLA_HEADS, hd)
    wq = jnp.pad(wq, ((0, 0), (0, 0), (0, 0), (0, MLA_PAD - hd)))
    wkv = w_kvb.reshape(depth, MLA_KV_RANK, MLA_HEADS, MLA_NOPE + MLA_V)
    wk = jnp.pad(wkv[..., :MLA_NOPE], ((0, 0), (0, 0), (0, 0), (0, MLA_PAD - MLA_NOPE)))
    wv = wkv[..., MLA_NOPE:].reshape(depth, MLA_KV_RANK, MLA_HEADS * MLA_V)
    return (wq.reshape(depth, MLA_Q_RANK, MLA_HEADS * MLA_PAD).astype(BF16),
            wk.reshape(depth, MLA_KV_RANK, MLA_HEADS * MLA_PAD).astype(BF16),
            jnp.swapaxes(wv, 1, 2).astype(BF16))


def _pack_gates(w_gates):
    depth = w_gates.shape[0]
    eye = jnp.eye(LRU_HEADS, dtype=w_gates.dtype)
    dense = jnp.einsum('ldghij,hk->ldghikj', w_gates, eye).reshape(depth, 2, 2, GROUP_W, GROUP_W)
    return jnp.concatenate([dense[:, :, 0], dense[:, :, 1]], axis=-1).astype(BF16)


def _rope_tables(rows, rot_dim, lane0, period, pad_rows):
    row = jnp.repeat(jnp.arange(rows, dtype=F32), GRID_W)
    col = jnp.tile(jnp.arange(GRID_W, dtype=F32), rows)
    per_axis = rot_dim // 2
    inv = ROPE_BASE ** (-jnp.arange(0, per_axis, 2, dtype=F32) / per_axis)
    ang = jnp.concatenate([row[:, None] * inv, col[:, None] * inv], axis=-1)
    cos, sin = jnp.cos(ang), jnp.sin(ang)
    lane = jnp.arange(LANES)
    rel = (lane - lane0) % period
    active = (lane >= lane0) & (rel < rot_dim)
    idx = rel % per_axis
    sign = jnp.where(rel < per_axis, -1.0, 1.0)
    cos_t = jnp.where(active[None, :], cos[:, idx], 1.0)
    sin_t = jnp.where(active[None, :], sin[:, idx] * sign[None, :], 0.0)
    cos_t = jnp.concatenate([cos_t, jnp.ones((pad_rows, LANES), F32)], axis=0)
    sin_t = jnp.concatenate([sin_t, jnp.zeros((pad_rows, LANES), F32)], axis=0)
    return cos_t, sin_t


def _pick(n, prefs):
    for p in prefs:
        if n % p == 0:
            return p
    return n


def kernel(x, c, ctx, c_ctx, w_mod, b_mod, norm_g, ffn_w_in, ffn_w_out, w_in, w_out,
           lru_conv_w, lru_conv_b, lru_w_gates, lru_b_gates, lru_lambda,
           mla_q_norm, mla_w_qb, mla_kv_norm, mla_w_kvb, diff_lambda, diff_norm, gqa_sink):
    batch, seq, d = x.shape
    ctx_len = ctx.shape[1]
    depth = w_mod.shape[0]
    d_ff = ffn_w_out.shape[2]
    n_lat, n_ctx = batch * seq, batch * ctx_len
    n_all = n_lat + n_ctx

    tm_ffn = _pick(math.gcd(seq, n_ctx), (1024, 512, 256))
    tf = _pick(d_ff, (512, 256, 128))
    tm_mix = _pick(math.gcd(seq, n_ctx), (512, 256, 128))
    tq = _pick(math.gcd(seq, ctx_len), (256, 128))
    tq_lat = _pick(seq, (512, 256, 128))
    tq_gqa = _pick(seq, (2 * WINDOW, WINDOW))
    tc = _pick(math.gcd(seq, ctx_len), (256, 128))
    tn_mod = _pick(N_MOD * d, (2048, 1024, 512, 256))
    assert seq % GRID_W == 0 and seq >= tq_gqa + 2 * WINDOW and ctx_len % WINDOW == 0

    def rows_fn(tm):
        per, nlb = seq // tm, n_lat // tm
        return lambda i: jnp.where(i < nlb, i // per, batch)

    def tab_fn(tm):
        per, nlb = seq // tm, n_lat // tm
        return lambda i: jnp.where(i < nlb, i % per, per)

    mod_rows = SUBLANES * ((batch + 1 + SUBLANES - 1) // SUBLANES)
    cvec = jnp.concatenate([c, c_ctx[None, :], jnp.zeros((mod_rows - batch - 1, d), F32)], axis=0)
    mod4 = _modulation(cvec, w_mod, b_mod, tn_mod).reshape(depth, mod_rows, 1, N_MOD * d)

    w_in_p, w_vt = _pack_w_in(w_in, _pick(d, (256, 128)))
    wq_p, wk_p, wmvt_p = _pack_mla(mla_w_qb, mla_w_kvb)
    mla_w = (mla_q_norm[:, None, :], wq_p, mla_kv_norm[:, None, :], wk_p, wmvt_p)
    wg_p = _pack_gates(lru_w_gates)
    norm4 = norm_g[:, :, None, :]
    conv_b3 = lru_conv_b[:, None, :]
    bg4 = lru_b_gates.reshape(depth, 2, 1, 2 * GROUP_W)
    lam4 = lru_lambda[:, :, None, :]
    diff_g3 = diff_norm[:, None, :]
    rows = seq // GRID_W
    tabs = (_rope_tables(rows, HEAD_DIM, 0, HEAD_DIM, tm_mix)
            + _rope_tables(rows, MLA_ROPE, MLA_NOPE, LANES, tm_mix))

    nb_lat, nb_ctx = n_lat // tm_ffn, n_ctx // tm_ffn
    xs = None
    for layer in range(depth):
        last = layer == depth - 1
        lam_init = 0.8 - 0.6 * math.exp(-0.3 * layer)
        ffn = functools.partial(_ffn_sublayer, w_in_all=ffn_w_in, w_out_all=ffn_w_out, layer=layer,
                                row_fn=rows_fn(tm_ffn), tm=tm_ffn, tf=tf, mod4=mod4, norm4=norm4)

        if layer == 0:
            xs = ffn(x.reshape(n_lat, d), nb_lat, n_all, 0, jnp.zeros((n_all, d), F32), k0=0, gi=0, sub=0)
            xs = ffn(ctx.reshape(n_ctx, d), nb_ctx, n_all, nb_lat, xs, k0=0, gi=0, sub=0)
        else:
            xs = ffn(xs, nb_lat + nb_ctx, n_all, 0, None, k0=0, gi=0, sub=0)

        (lru_cols, mq, mk, mvt, dq, dk, dvt, gq, gk, gvt) = _mixer_in(
            xs, mod4, norm4, w_in_p, w_vt, tabs, mla_w, layer, rows_fn(tm_mix), tab_fn(tm_mix), tm_mix)

        hf, hb = _lru_mixer(lru_cols, lru_conv_w, conv_b3, wg_p, bg4, lam4, layer,
                            batch, seq, ctx_len, tc)

        diff_extra = [diff_lambda, diff_g3]
        diff_specs = [_layer_spec(diff_lambda, layer), _layer_spec(diff_g3, layer)]
        diff_k = functools.partial(_diff_kernel, lam_init=lam_init)
        n_out = n_lat if last else n_all
        fresh = lambda: None if last else jnp.zeros((n_all, GROUP_W), BF16)
        yb = _full_attention(_mla_kernel, mq, mk, mvt, [], [], batch, seq, ctx_len, tq_lat,
                             False, fresh(), n_out)
        yc = _full_attention(diff_k, dq, dk, dvt, diff_specs, diff_extra, batch, seq, ctx_len, tq_lat,
                             False, fresh(), n_out)
        yd = _gqa_attention(gq, gk, gvt, gqa_sink, layer, batch, seq, ctx_len, tq_gqa, False, fresh(), n_out)
        if not last:
            yb = _full_attention(_mla_kernel, mq, mk, mvt, [], [], batch, seq, ctx_len, tq, True, yb, n_out)
            yc = _full_attention(diff_k, dq, dk, dvt, diff_specs, diff_extra, batch, seq, ctx_len, tq,
                                 True, yc, n_out)
            yd = _gqa_attention(gq, gk, gvt, gqa_sink, layer, batch, seq, ctx_len, tq, True, yd, n_out)

        xs = _mixer_out(xs, n_out, mod4, norm4, hf, hb, lru_cols, yb, yc, yd, w_out, layer,
                        rows_fn(tm_mix), tm_mix)
        xs = ffn(xs, n_out // tm_ffn, n_out, 0, None, k0=6, gi=4, sub=1)

    return xs.reshape(batch, seq, d)
```

```python
import functools
import math

import jax
import jax.numpy as jnp
from jax import lax
from jax.experimental import pallas as pl
from jax.experimental.pallas import tpu as pltpu

F32 = jnp.float32
BF16 = jnp.bfloat16

GRID_W = 64
HEAD_DIM = 64
N_MOD = 9
GROUP_W = 512
LRU_HEADS = 8
LRU_BLOCK = 64
LRU_C = 8.0
MLA_HEADS = 8
MLA_NOPE = 64
MLA_ROPE = 32
MLA_V = 64
MLA_Q_RANK = 384
MLA_KV_RANK = 256
DIFF_HEADS = 4
DIFF_D = 64
DIFF_V = 128
GQA_Q_HEADS = 8
GQA_KV_HEADS = 2
GQA_GROUP = 4
WINDOW = 128
MACARON_WEIGHT = 0.5
ROPE_BASE = 10000.0
RMS_EPS = 1e-6
NEG_INF = -1e30
LOG2E = 1.4426950408889634

LANES = 128
SUBLANES = 8
MLA_PAD = 128

COLS_LRU = 2 * GROUP_W
COLS_MLA = MLA_Q_RANK + MLA_KV_RANK + MLA_ROPE
DIFF_QK = 2 * DIFF_HEADS * 2 * DIFF_D
COLS_DIFF = DIFF_QK + DIFF_HEADS * DIFF_V
GQA_QK = (GQA_Q_HEADS + GQA_KV_HEADS) * HEAD_DIM
GQA_KVW = GQA_KV_HEADS * HEAD_DIM
COLS_GQA = GQA_QK + GQA_KVW

VMEM_LIMIT = 60 * 1024 * 1024


def _cparams(sem):
    return pltpu.CompilerParams(dimension_semantics=sem, vmem_limit_bytes=VMEM_LIMIT)


def _rms(x, g):
    ms = jnp.mean(x * x, axis=-1, keepdims=True)
    return x * lax.rsqrt(ms + RMS_EPS) * g


def _dot(a, b):
    return jnp.dot(a, b, preferred_element_type=F32)


def _dot_nt(a, b):
    return lax.dot_general(a, b, (((1,), (1,)), ((), ())), preferred_element_type=F32)


def _layer_spec(arr, layer, *lead):
    shape = arr.shape[1 + len(lead):]
    return pl.BlockSpec((None,) * (1 + len(lead)) + shape,
                        lambda *_: (layer,) + lead + (0,) * len(shape),
                        pipeline_mode=pl.Buffered(1))


def _mod_spec(d, layer, row_fn, k):
    return pl.BlockSpec((None, None, 1, d), lambda i, *_: (layer, row_fn(i), 0, k))


def _skip_first(kernel):
    def wrapped(_, *refs, **kw):
        return kernel(*refs, **kw)
    return wrapped


def _mod_kernel(c_ref, w_ref, b_ref, o_ref):
    s = c_ref[...]
    s = s * jax.nn.sigmoid(s)
    o_ref[...] = _dot(s.astype(BF16), w_ref[...].astype(BF16)) + b_ref[...]


def _modulation(cvec, w_mod, b_mod, tn):
    depth, d, n = w_mod.shape
    rows = cvec.shape[0]
    return pl.pallas_call(
        _mod_kernel,
        grid=(depth, n // tn),
        in_specs=[
            pl.BlockSpec((rows, d), lambda l, j: (0, 0)),
            pl.BlockSpec((None, d, tn), lambda l, j: (l, 0, j)),
            pl.BlockSpec((None, 1, tn), lambda l, j: (l, 0, j)),
        ],
        out_specs=pl.BlockSpec((None, rows, tn), lambda l, j: (l, 0, j)),
        out_shape=jax.ShapeDtypeStruct((depth, rows, n), F32),
        compiler_params=_cparams(("arbitrary", "arbitrary")),
        name="modulation",
    )(cvec, w_mod, b_mod.reshape(depth, 1, n))


FFN_ROW_CHUNK = 64


def _ffn_kernel(x_ref, shift_ref, scale_ref, gate_ref, gpre_ref, gpost_ref,
                wg_ref, wu_ref, wo_ref, o_ref, h_ref, *, nj):
    j = pl.program_id(1)
    n_chunks = x_ref.shape[0] // FFN_ROW_CHUNK

    def rows(r):
        return pl.ds(pl.multiple_of(r * FFN_ROW_CHUNK, FFN_ROW_CHUNK), FFN_ROW_CHUNK)

    @pl.when(j == 0)
    def _():
        mul = gpre_ref[...] * (1.0 + scale_ref[...])
        shift = shift_ref[...]

        def body(r, carry):
            x = x_ref[rows(r), :]
            ms = jnp.mean(x * x, axis=-1, keepdims=True)
            h_ref[rows(r), :] = (x * lax.rsqrt(ms + RMS_EPS) * mul + shift).astype(BF16)
            return carry

        lax.fori_loop(0, n_chunks, body, 0, unroll=2)
        o_ref[...] = jnp.zeros_like(o_ref)

    h = h_ref[...]
    g = _dot(h, wg_ref[...].astype(BF16))
    u = _dot(h, wu_ref[...].astype(BF16))
    a = (g * jax.nn.sigmoid(g) * u).astype(BF16)
    o_ref[...] += _dot(a, wo_ref[...].astype(BF16))

    @pl.when(j == nj - 1)
    def _():
        mul = (MACARON_WEIGHT * gate_ref[...]) * gpost_ref[...]

        def inv_rms(r):
            y = o_ref[rows(r), :]
            return lax.rsqrt(jnp.mean(y * y, axis=-1, keepdims=True) + RMS_EPS)

        def body(r, inv):
            inv_next = inv_rms(jnp.minimum(r + 1, n_chunks - 1))
            o_ref[rows(r), :] = x_ref[rows(r), :] + o_ref[rows(r), :] * inv * mul
            return inv_next

        lax.fori_loop(0, n_chunks, body, inv_rms(0))


def _ffn_sublayer(x_in, n_blocks, out_rows, out_blk0, prev_out, mod4, k0, norm4, gi,
                  w_in_all, w_out_all, layer, sub, row_fn, tm, tf):
    d = x_in.shape[1]
    d_ff = w_out_all.shape[2]
    nj = d_ff // tf
    out_row = lambda i: row_fn(out_blk0 + i)
    g_spec = lambda idx: pl.BlockSpec((None, None, 1, d), lambda i, j: (layer, idx, 0, 0))
    lead_specs, lead_args, aliases, body = [], [], {}, _ffn_kernel
    if prev_out is not None:
        lead_specs, lead_args, aliases = [pl.BlockSpec(memory_space=pl.ANY)], [prev_out], {0: 0}
        body = _skip_first(_ffn_kernel)
    return pl.pallas_call(
        functools.partial(body, nj=nj),
        grid=(n_blocks, nj),
        in_specs=lead_specs + [
            pl.BlockSpec((tm, d), lambda i, j: (i, 0), pipeline_mode=pl.Buffered(1)),
            _mod_spec(d, layer, out_row, k0), _mod_spec(d, layer, out_row, k0 + 1),
            _mod_spec(d, layer, out_row, k0 + 2),
            g_spec(gi), g_spec(gi + 1),
            pl.BlockSpec((None, None, d, tf), lambda i, j: (layer, sub, 0, j)),
            pl.BlockSpec((None, None, d, tf), lambda i, j: (layer, sub, 0, nj + j)),
            pl.BlockSpec((None, None, tf, d), lambda i, j: (layer, sub, j, 0)),
        ],
        out_specs=pl.BlockSpec((tm, d), lambda i, j: (out_blk0 + i, 0), pipeline_mode=pl.Buffered(1)),
        out_shape=jax.ShapeDtypeStruct((out_rows, d), F32),
        scratch_shapes=[pltpu.VMEM((tm, d), BF16)],
        input_output_aliases=aliases,
        compiler_params=_cparams(("arbitrary", "arbitrary")),
        name="ffn_sublayer",
    )(*lead_args, x_in, mod4, mod4, mod4, norm4, norm4, w_in_all, w_in_all, w_out_all)


def _swap_halves(x, first, half):
    return jnp.where(first, pltpu.roll(x, LANES - half, 1), pltpu.roll(x, half, 1))


def _rope_chunks(x, cos, sin, first, half):
    out = []
    for c in range(x.shape[1] // LANES):
        blk = x[:, c * LANES:(c + 1) * LANES]
        out.append(blk * cos + _swap_halves(blk, first, half) * sin)
    return out[0] if len(out) == 1 else jnp.concatenate(out, axis=1)


def _mixer_in_kernel(x_ref, shift_ref, scale_ref, g_ref, wt_ref,
                     cos_hd_ref, sin_hd_ref, cos_mla_ref, sin_mla_ref,
                     qn_ref, wq_ref, kvn_ref, wk_ref, wmvt_ref,
                     lru_ref, mq_ref, mk_ref, mvt_ref, dq_ref, dk_ref, dvt_ref,
                     gq_ref, gk_ref, gvt_ref):
    h = (_rms(x_ref[...], g_ref[...]) * (1.0 + scale_ref[...]) + shift_ref[...]).astype(BF16)
    tm = h.shape[0]
    lane = lax.broadcasted_iota(jnp.int32, (tm, LANES), 1)
    first_hd = (lane % HEAD_DIM) < (HEAD_DIM // 2)
    first_mla = lane < (MLA_NOPE + MLA_ROPE // 2)
    cos_hd, sin_hd = cos_hd_ref[...], sin_hd_ref[...]
    cos_mla, sin_mla = cos_mla_ref[...], sin_mla_ref[...]

    o_mla = COLS_LRU
    o_ckv = o_mla + MLA_Q_RANK
    o_kr = o_ckv + MLA_KV_RANK
    o_diff = o_mla + COLS_MLA
    o_gqa = o_diff + COLS_DIFF
    proj = lambda a, b: _dot_nt(h, wt_ref[a:b, :])

    lru_ref[...] = proj(0, COLS_LRU)

    qn = _rms(proj(o_mla, o_ckv), qn_ref[...]).astype(BF16)
    q = _rope_chunks(_dot(qn, wq_ref[...]), cos_mla, sin_mla, first_mla, MLA_ROPE // 2)
    mq_ref[...] = (q * (LOG2E * (MLA_NOPE + MLA_ROPE) ** -0.5)).astype(BF16)
    kvn = _rms(proj(o_ckv, o_kr), kvn_ref[...]).astype(BF16)
    kr = pltpu.roll(proj(o_kr, o_kr + LANES), MLA_NOPE, 1)
    kr = jnp.where((lane >= MLA_NOPE) & (lane < MLA_NOPE + MLA_ROPE), kr, 0.0)
    k_rope = _rope_chunks(kr, cos_mla, sin_mla, first_mla, MLA_ROPE // 2)
    k_nope = _dot(kvn, wk_ref[...])
    mk_ref[...] = (k_nope + jnp.concatenate([k_rope] * MLA_HEADS, axis=1)).astype(BF16)
    mvt_ref[...] = _dot_nt(wmvt_ref[...], kvn).astype(BF16)

    cd = proj(o_diff, o_diff + DIFF_QK)
    half = DIFF_QK // 2
    dq_ref[...] = (_rope_chunks(cd[:, :half], cos_hd, sin_hd, first_hd, HEAD_DIM // 2)
                   * (LOG2E * DIFF_D ** -0.5)).astype(BF16)
    dk_ref[...] = _rope_chunks(cd[:, half:], cos_hd, sin_hd, first_hd, HEAD_DIM // 2).astype(BF16)

    nq = GQA_Q_HEADS * HEAD_DIM
    cg = proj(o_gqa, o_gqa + GQA_QK)
    gq_ref[...] = (_rope_chunks(cg[:, :nq], cos_hd, sin_hd, first_hd, HEAD_DIM // 2)
                   * (LOG2E * HEAD_DIM ** -0.5)).astype(BF16)
    gk_ref[...] = _rope_chunks(cg[:, nq:], cos_hd, sin_hd, first_hd, HEAD_DIM // 2).astype(BF16)

    dvt_ref[...] = _dot_nt(wt_ref[o_diff + DIFF_QK:o_gqa, :], h).astype(BF16)
    gvt_ref[...] = _dot_nt(wt_ref[o_gqa + GQA_QK:o_gqa + COLS_GQA, :], h).astype(BF16)


def _mixer_in(xs, mod4, norm4, w_in_t, tabs, mla_w, layer, row_fn, tab_fn, tm):
    n_rows, d = xs.shape
    cos_hd, sin_hd, cos_mla, sin_mla = tabs
    qn, wq, kvn, wk, wmvt = mla_w
    tab_spec = pl.BlockSpec((tm, LANES), lambda i: (tab_fn(i), 0))
    outs = [(COLS_LRU, F32, False),
            (MLA_HEADS * MLA_PAD, BF16, False), (MLA_HEADS * MLA_PAD, BF16, False),
            (GROUP_W, BF16, True),
            (GROUP_W, BF16, False), (GROUP_W, BF16, False), (GROUP_W, BF16, True),
            (GROUP_W, BF16, False), (GQA_KVW, BF16, False), (GQA_KVW, BF16, True)]
    out_specs = [pl.BlockSpec((w, tm), lambda i: (0, i)) if t else pl.BlockSpec((tm, w), lambda i: (i, 0))
                 for w, _, t in outs]
    out_shape = [jax.ShapeDtypeStruct((w, n_rows) if t else (n_rows, w), dt) for w, dt, t in outs]
    return pl.pallas_call(
        _mixer_in_kernel,
        grid=(n_rows // tm,),
        in_specs=[
            pl.BlockSpec((tm, d), lambda i: (i, 0)),
            _mod_spec(d, layer, row_fn, 3), _mod_spec(d, layer, row_fn, 4),
            _layer_spec(norm4, layer, 2), _layer_spec(w_in_t, layer),
            tab_spec, tab_spec, tab_spec, tab_spec,
            _layer_spec(qn, layer), _layer_spec(wq, layer), _layer_spec(kvn, layer),
            _layer_spec(wk, layer), _layer_spec(wmvt, layer),
        ],
        out_specs=out_specs,
        out_shape=out_shape,
        compiler_params=_cparams(("arbitrary",)),
        name="mixer_in",
    )(xs, mod4, mod4, norm4, w_in_t, cos_hd, sin_hd, cos_mla, sin_mla, qn, wq, kvn, wk, wmvt)


def _sigmoid(x):
    return 0.5 * jnp.tanh(0.5 * x) + 0.5


def _lru_direction(x, prev, nxt, cw, cb, wg, bg, lam, reverse, carry_ref, h_ref, a_scr, b_scr):
    tc, ch = x.shape
    ext = jnp.concatenate([prev, x, nxt], axis=0)
    n_ext = tc + 2 * SUBLANES

    def tap(shift):
        return pltpu.roll(ext, (-shift) % n_ext, 0)[SUBLANES:SUBLANES + tc]

    u = cw[0:1] * tap(-2) + cw[1:2] * tap(-1) + cw[2:3] * x + cw[3:4] * tap(1) + cb
    g = _dot(u.astype(BF16), wg) + bg
    r = _sigmoid(g[:, :ch])
    i = _sigmoid(g[:, ch:])
    log_a = -LRU_C * r * jax.nn.softplus(-lam)
    a = jnp.exp(log_a)
    b = jnp.sqrt((1.0 - a) * (1.0 + a)) * (i * u)

    nt = tc // SUBLANES
    a = a.reshape(nt, SUBLANES, ch)
    b = b.reshape(nt, SUBLANES, ch)
    row = lax.broadcasted_iota(jnp.int32, (nt, SUBLANES, ch), 1)
    for s in (1, 2, 4):
        ok = (row < SUBLANES - s) if reverse else (row >= s)
        shift = SUBLANES - s if reverse else s
        a_sh, b_sh = pltpu.roll(a, shift, 1), pltpu.roll(b, shift, 1)
        b = a * jnp.where(ok, b_sh, 0.0) + b
        a = a * jnp.where(ok, a_sh, 1.0)
    a_scr[...] = a.reshape(tc, ch)
    b_scr[...] = b.reshape(tc, ch)

    def body(k, carry):
        kk = nt - 1 - k if reverse else k
        sl = pl.ds(pl.multiple_of(kk * SUBLANES, SUBLANES), SUBLANES)
        ht = b_scr[sl, :] + a_scr[sl, :] * carry
        h_ref[sl, :] = ht
        edge = ht[0:1, :] if reverse else ht[SUBLANES - 1:SUBLANES, :]
        return jnp.broadcast_to(edge, (SUBLANES, ch))

    carry_ref[...] = lax.fori_loop(0, nt, body, carry_ref[...])


def _lru_kernel(xf_ref, pf_ref, nf_ref, xb_ref, pb_ref, nb_ref,
                cw_ref, cb_ref, wgf_ref, wgb_ref, bgf_ref, bgb_ref, lamf_ref, lamb_ref,
                hf_ref, hb_ref, cf_ref, cbk_ref, a_scr, b_scr, *, halo_fn):
    s = pl.program_id(1)

    @pl.when(s == 0)
    def _():
        cf_ref[...] = jnp.zeros_like(cf_ref)
        cbk_ref[...] = jnp.zeros_like(cbk_ref)

    pvf, nvf, pvb, nvb = halo_fn(s)
    cw, cb = cw_ref[...], cb_ref[...]
    _lru_direction(xf_ref[...], pf_ref[...] * pvf, nf_ref[...] * nvf, cw, cb,
                   wgf_ref[...], bgf_ref[...], lamf_ref[...], False, cf_ref, hf_ref, a_scr, b_scr)
    _lru_direction(xb_ref[...], pb_ref[...] * pvb, nb_ref[...] * nvb, cw, cb,
                   wgb_ref[...], bgb_ref[...], lamb_ref[...], True, cbk_ref, hb_ref, a_scr, b_scr)


def _lru_mixer(lru_cols, conv_w, conv_b3, wg, bg4, lam4, layer, batch, seq, ctx_len, tc):
    n_rows = lru_cols.shape[0]
    ch = GROUP_W
    nl, ncx = seq // tc, ctx_len // tc
    n_steps = nl + ncx
    r8 = tc // SUBLANES
    last8 = n_rows // SUBLANES - 1

    def chunk_f(b, s):
        return jnp.where(s < ncx, batch * nl + b * ncx + s, b * nl + s - ncx)

    def chunk_b(b, s):
        return jnp.where(s < ncx, batch * nl + b * ncx + ncx - 1 - s, b * nl + nl - 1 - (s - ncx))

    def seg_pos(s, reverse):
        in_ctx = s < ncx
        if reverse:
            idx = jnp.where(in_ctx, ncx - 1 - s, nl - 1 - (s - ncx))
        else:
            idx = jnp.where(in_ctx, s, s - ncx)
        return idx, jnp.where(in_ctx, ncx, nl)

    def halo_fn(s):
        out = []
        for reverse in (False, True):
            idx, n = seg_pos(s, reverse)
            out += [(idx > 0).astype(F32), (idx < n - 1).astype(F32)]
        return out

    def x_spec(fn):
        return pl.BlockSpec((tc, ch), lambda b, s: (fn(b, s), 0))

    def prev_spec(fn):
        return pl.BlockSpec((SUBLANES, ch), lambda b, s: (jnp.maximum(fn(b, s) * r8 - 1, 0), 0))

    def next_spec(fn):
        return pl.BlockSpec((SUBLANES, ch), lambda b, s: (jnp.minimum((fn(b, s) + 1) * r8, last8), 0))

    return pl.pallas_call(
        functools.partial(_lru_kernel, halo_fn=halo_fn),
        grid=(batch, n_steps),
        in_specs=[
            x_spec(chunk_f), prev_spec(chunk_f), next_spec(chunk_f),
            x_spec(chunk_b), prev_spec(chunk_b), next_spec(chunk_b),
            _layer_spec(conv_w, layer), _layer_spec(conv_b3, layer),
            _layer_spec(wg, layer, 0), _layer_spec(wg, layer, 1),
            _layer_spec(bg4, layer, 0), _layer_spec(bg4, layer, 1),
            _layer_spec(lam4, layer, 0), _layer_spec(lam4, layer, 1),
        ],
        out_specs=[x_spec(chunk_f), x_spec(chunk_b)],
        out_shape=[jax.ShapeDtypeStruct((n_rows, ch), F32)] * 2,
        scratch_shapes=[pltpu.VMEM((SUBLANES, ch), F32), pltpu.VMEM((SUBLANES, ch), F32),
                        pltpu.VMEM((tc, ch), F32), pltpu.VMEM((tc, ch), F32)],
        compiler_params=_cparams(("arbitrary", "arbitrary")),
        name="rglru",
    )(lru_cols, lru_cols, lru_cols, lru_cols, lru_cols, lru_cols,
      conv_w, conv_b3, wg, wg, bg4, bg4, lam4, lam4)


def _softmax_cols(scores, extra=None):
    m = scores[0].max(axis=0, keepdims=True)
    for s in scores[1:]:
        m = jnp.maximum(m, s.max(axis=0, keepdims=True))
    if extra is not None:
        m = jnp.maximum(m, extra)
    ps = [jnp.exp2(s - m) for s in scores]
    den = ps[0].sum(axis=0, keepdims=True)
    for p in ps[1:]:
        den = den + p.sum(axis=0, keepdims=True)
    if extra is not None:
        den = den + jnp.exp2(extra - m)
    return ps, den


def _two_block_queries(q2, width):
    lane = lax.broadcasted_iota(jnp.int32, q2.shape, 1)
    zero = jnp.zeros((), q2.dtype)
    return jnp.concatenate([jnp.where(lane < width, q2, zero), jnp.where(lane >= width, q2, zero)], axis=0)


def _paired_probs(q_ref, ks, width, n_iter, finish, s_bufs):
    def issue(i):
        sl = slice(2 * i * width, 2 * (i + 1) * width)
        qq = _two_block_queries(q_ref[:, sl], width)
        buf, row, m = s_bufs[i % 2], 0, None
        for k in ks:
            s = _dot_nt(k[:, sl], qq)
            buf[row:row + k.shape[0], :] = s
            row += k.shape[0]
            ms = s.max(axis=0, keepdims=True)
            m = ms if m is None else jnp.maximum(m, ms)
        return m

    m_next = issue(0)
    for i in range(n_iter):
        m = m_next
        if i + 1 < n_iter:
            m_next = issue(i + 1)
        p = jnp.exp2(s_bufs[i % 2][...] - m)
        finish(i, p, p.sum(axis=0, keepdims=True))


def _weighted_values(vts, rows, p):
    o, row = None, 0
    for vt in vts:
        t = _dot(vt[rows, :], p[row:row + vt.shape[1], :])
        o = t if o is None else o + t
        row += vt.shape[1]
    return o


def _mla_kernel(q_ref, *refs, n_seg):
    ks, vts = refs[:n_seg], refs[n_seg:2 * n_seg]
    o_ref, s_bufs = refs[2 * n_seg], refs[2 * n_seg + 1:]
    tq = q_ref.shape[0]
    outs = []

    def finish(hp, p, den):
        p = p.astype(BF16)
        for e in range(2):
            h = 2 * hp + e
            cols = slice(e * tq, (e + 1) * tq)
            o = _weighted_values(vts, slice(h * MLA_V, (h + 1) * MLA_V), p[:, cols])
            outs.append(o / den[:, cols])

    _paired_probs(q_ref, ks, MLA_PAD, MLA_HEADS // 2, finish, s_bufs)
    o_ref[...] = jnp.concatenate(outs, axis=0).T.astype(o_ref.dtype)


def _diff_kernel(q_ref, lv_ref, g_ref, *refs, n_seg, lam_init):
    ks, vts = refs[:n_seg], refs[n_seg:2 * n_seg]
    o_ref, s_bufs = refs[2 * n_seg], refs[2 * n_seg + 1:]
    lv = lv_ref[...]
    lam = (jnp.exp(jnp.sum(lv[0:1] * lv[1:2], axis=-1, keepdims=True))
           - jnp.exp(jnp.sum(lv[2:3] * lv[3:4], axis=-1, keepdims=True)) + lam_init)
    tq = q_ref.shape[0]
    outs = []

    def finish(h, p, den):
        c1 = 1.0 / den[:, :tq]
        c2 = lam / den[:, tq:]
        w = (p[:, :tq] * c1 - p[:, tq:] * c2).astype(BF16)
        outs.append(_weighted_values(vts, slice(h * DIFF_V, (h + 1) * DIFF_V), w))

    _paired_probs(q_ref, ks, DIFF_D, DIFF_HEADS, finish, s_bufs)
    o = jnp.concatenate(outs, axis=0).T
    g = g_ref[...]
    for h in range(DIFF_HEADS):
        sl = slice(h * DIFF_V, (h + 1) * DIFF_V)
        o_ref[:, sl] = (_rms(o[:, sl], g) * (1.0 - lam_init)).astype(o_ref.dtype)


def _into(buf):
    if buf is None:
        return [], [], {}
    return [pl.BlockSpec(memory_space=pl.ANY)], [buf], {0: 0}


def _full_attention(kernel, q, k, vt, extra_specs, extra, batch, seq, ctx_len, tq,
                    ctx_queries, into, out_rows):
    qw, kw, vw = q.shape[1], k.shape[1], vt.shape[0]
    ctx_blk0 = batch * seq // ctx_len

    ctx_k = pl.BlockSpec((ctx_len, kw), lambda b, i: (ctx_blk0 + b, 0))
    ctx_v = pl.BlockSpec((vw, ctx_len), lambda b, i: (0, ctx_blk0 + b))
    lat_k = pl.BlockSpec((seq, kw), lambda b, i: (b, 0))
    lat_v = pl.BlockSpec((vw, seq), lambda b, i: (0, b))
    if ctx_queries:
        nq = ctx_len // tq
        q_map = lambda b, i: (batch * seq // tq + b * nq + i, 0)
        kv_specs, kv_args, n_seg = [ctx_k, ctx_v], [k, vt], 1
    else:
        nq = seq // tq
        q_map = lambda b, i: (b * nq + i, 0)
        kv_specs, kv_args, n_seg = [lat_k, ctx_k, lat_v, ctx_v], [k, k, vt, vt], 2
    lead_specs, lead_args, aliases = _into(into)
    body = functools.partial(kernel if into is None else _skip_first(kernel), n_seg=n_seg)
    n_keys = ctx_len if ctx_queries else seq + ctx_len
    return pl.pallas_call(
        body,
        grid=(batch, nq),
        in_specs=lead_specs + [pl.BlockSpec((tq, qw), q_map)] + extra_specs + kv_specs,
        out_specs=pl.BlockSpec((tq, GROUP_W), q_map),
        out_shape=jax.ShapeDtypeStruct((out_rows, GROUP_W), BF16),
        scratch_shapes=[pltpu.VMEM((n_keys, 2 * tq), F32)] * 2,
        input_output_aliases=aliases,
        compiler_params=_cparams(("arbitrary", "arbitrary")),
        name="full_attention",
    )(*lead_args, q, *extra, *kv_args)


def _gqa_group(q_ref, kv, ks, vts, masks, sinks, outs):
    ksl = slice(kv * HEAD_DIM, (kv + 1) * HEAD_DIM)
    heads = range(kv * GQA_GROUP, (kv + 1) * GQA_GROUP)
    nq = q_ref.shape[0]
    q4 = jnp.concatenate([q_ref[:, h * HEAD_DIM:(h + 1) * HEAD_DIM] for h in heads], axis=0)
    scores = []
    for k, mask in zip(ks, masks):
        s = _dot_nt(k[:, ksl], q4)
        scores.append(s if mask is None else jnp.where(mask, s, NEG_INF))
    sink = jnp.concatenate([jnp.full((1, nq), sinks(h) * LOG2E, F32) for h in heads], axis=1)
    ps, den = _softmax_cols(scores, extra=sink)
    o = None
    for p, vt in zip(ps, vts):
        t = _dot(vt[ksl, :], p.astype(BF16))
        o = t if o is None else o + t
    o = o / den
    outs += [o[:, g * nq:(g + 1) * nq] for g in range(GQA_GROUP)]


def _gqa_lat_kernel(sink_ref, q_ref, kl_ref, vtl_ref, kc_ref, vtc_ref, o_ref, *, seq, layer):
    n = pl.program_id(1)
    tq = q_ref.shape[0]
    span = tq + 2 * WINDOW
    start = pl.multiple_of(jnp.clip(n * tq - WINDOW, 0, seq - span), WINDOW)
    kw = kl_ref[pl.ds(start, span), :]
    vtw = vtl_ref[:, pl.ds(start, span)]
    kpos = start + lax.broadcasted_iota(jnp.int32, (span, GQA_GROUP * tq), 0)
    qpos = n * tq + lax.broadcasted_iota(jnp.int32, (span, GQA_GROUP * tq), 1) % tq
    valid = jnp.abs(kpos - qpos) <= WINDOW
    sinks = lambda h: sink_ref[layer, h]
    outs = []
    for kv in range(GQA_KV_HEADS):
        _gqa_group(q_ref, kv, [kc_ref[...], kw], [vtc_ref[...], vtw], [None, valid], sinks, outs)
    o_ref[...] = jnp.concatenate(outs, axis=0).T.astype(o_ref.dtype)


def _gqa_ctx_kernel(sink_ref, q_ref, kc_ref, vtc_ref, o_ref, *, layer):
    sinks = lambda h: sink_ref[layer, h]
    outs = []
    for kv in range(GQA_KV_HEADS):
        _gqa_group(q_ref, kv, [kc_ref[...]], [vtc_ref[...]], [None], sinks, outs)
    o_ref[...] = jnp.concatenate(outs, axis=0).T.astype(o_ref.dtype)


def _gqa_attention(q, k, vt, sink, layer, batch, seq, ctx_len, tq, ctx_queries, into, out_rows):
    ctx_blk0 = batch * seq // ctx_len
    smem = pl.BlockSpec(memory_space=pltpu.SMEM)
    ctx_k = pl.BlockSpec((ctx_len, GQA_KVW), lambda b, i: (ctx_blk0 + b, 0))
    ctx_v = pl.BlockSpec((GQA_KVW, ctx_len), lambda b, i: (0, ctx_blk0 + b))
    if ctx_queries:
        nq = ctx_len // tq
        q_map = lambda b, i: (batch * seq // tq + b * nq + i, 0)
        kernel = functools.partial(_gqa_ctx_kernel, layer=layer)
        specs = [smem, pl.BlockSpec((tq, GROUP_W), q_map), ctx_k, ctx_v]
        args = [sink, q, k, vt]
    else:
        nq = seq // tq
        q_map = lambda b, i: (b * nq + i, 0)
        lat_k = pl.BlockSpec((seq, GQA_KVW), lambda b, i: (b, 0))
        lat_v = pl.BlockSpec((GQA_KVW, seq), lambda b, i: (0, b))
        kernel = functools.partial(_gqa_lat_kernel, seq=seq, layer=layer)
        specs = [smem, pl.BlockSpec((tq, GROUP_W), q_map), lat_k, lat_v, ctx_k, ctx_v]
        args = [sink, q, k, vt, k, vt]
    lead_specs, lead_args, aliases = _into(into)
    if into is not None:
        kernel = _skip_first(kernel)
    return pl.pallas_call(
        kernel,
        grid=(batch, nq),
        in_specs=lead_specs + specs,
        out_specs=pl.BlockSpec((tq, GROUP_W), q_map),
        out_shape=jax.ShapeDtypeStruct((out_rows, GROUP_W), BF16),
        input_output_aliases=aliases,
        compiler_params=_cparams(("arbitrary", "arbitrary")),
        name="gqa_attention",
    )(*lead_args, *args)


def _mixer_out_kernel(x_ref, gate_ref, g_ref, hf_ref, hb_ref, lg_ref, yb_ref, yc_ref, yd_ref,
                      w_ref, o_ref):
    ya = ((hf_ref[...] + hb_ref[...]) * jax.nn.gelu(lg_ref[...])).astype(BF16)
    w = lambda g: w_ref[g * GROUP_W:(g + 1) * GROUP_W, :].astype(BF16)
    y = _dot(ya, w(0))
    y += _dot(yb_ref[...], w(1))
    y += _dot(yc_ref[...], w(2))
    y += _dot(yd_ref[...], w(3))
    o_ref[...] = x_ref[...] + gate_ref[...] * _rms(y, g_ref[...])


def _mixer_out(xs, n_rows, mod4, norm4, hf, hb, lru_cols, yb, yc, yd, w_out, layer, row_fn, tm):
    d = xs.shape[1]
    row_spec = pl.BlockSpec((tm, GROUP_W), lambda i: (i, 0))
    return pl.pallas_call(
        _mixer_out_kernel,
        grid=(n_rows // tm,),
        in_specs=[
            pl.BlockSpec((tm, d), lambda i: (i, 0)),
            _mod_spec(d, layer, row_fn, 5), _layer_spec(norm4, layer, 3),
            row_spec, row_spec,
            pl.BlockSpec((tm, GROUP_W), lambda i: (i, 1)),
            row_spec, row_spec, row_spec,
            _layer_spec(w_out, layer),
        ],
        out_specs=pl.BlockSpec((tm, d), lambda i: (i, 0)),
        out_shape=jax.ShapeDtypeStruct((n_rows, d), F32),
        compiler_params=_cparams(("arbitrary",)),
        name="mixer_out",
    )(xs, mod4, norm4, hf, hb, lru_cols, yb, yc, yd, w_out)


def _pack_mla(w_qb, w_kvb):
    depth = w_qb.shape[0]
    hd = MLA_NOPE + MLA_ROPE
    wq = w_qb.reshape(depth, MLA_Q_RANK, MLA_HEADS, hd)
    wq = jnp.pad(wq, ((0, 0), (0, 0), (0, 0), (0, MLA_PAD - hd)))
    wkv = w_kvb.reshape(depth, MLA_KV_RANK, MLA_HEADS, MLA_NOPE + MLA_V)
    wk = jnp.pad(wkv[..., :MLA_NOPE], ((0, 0), (0, 0), (0, 0), (0, MLA_PAD - MLA_NOPE)))
    wv = wkv[..., MLA_NOPE:].reshape(depth, MLA_KV_RANK, MLA_HEADS * MLA_V)
    return (wq.reshape(depth, MLA_Q_RANK, MLA_HEADS * MLA_PAD).astype(BF16),
            wk.reshape(depth, MLA_KV_RANK, MLA_HEADS * MLA_PAD).astype(BF16),
            jnp.swapaxes(wv, 1, 2).astype(BF16))


def _pack_gates(w_gates):
    depth = w_gates.shape[0]
    eye = jnp.eye(LRU_HEADS, dtype=w_gates.dtype)
    dense = jnp.einsum('ldghij,hk->ldghikj', w_gates, eye).reshape(depth, 2, 2, GROUP_W, GROUP_W)
    return jnp.concatenate([dense[:, :, 0], dense[:, :, 1]], axis=-1).astype(BF16)


def _rope_tables(rows, rot_dim, lane0, period, pad_rows):
    row = jnp.repeat(jnp.arange(rows, dtype=F32), GRID_W)
    col = jnp.tile(jnp.arange(GRID_W, dtype=F32), rows)
    per_axis = rot_dim // 2
    inv = ROPE_BASE ** (-jnp.arange(0, per_axis, 2, dtype=F32) / per_axis)
    ang = jnp.concatenate([row[:, None] * inv, col[:, None] * inv], axis=-1)
    cos, sin = jnp.cos(ang), jnp.sin(ang)
    lane = jnp.arange(LANES)
    rel = (lane - lane0) % period
    active = (lane >= lane0) & (rel < rot_dim)
    idx = rel % per_axis
    sign = jnp.where(rel < per_axis, -1.0, 1.0)
    cos_t = jnp.where(active[None, :], cos[:, idx], 1.0)
    sin_t = jnp.where(active[None, :], sin[:, idx] * sign[None, :], 0.0)
    cos_t = jnp.concatenate([cos_t, jnp.ones((pad_rows, LANES), F32)], axis=0)
    sin_t = jnp.concatenate([sin_t, jnp.zeros((pad_rows, LANES), F32)], axis=0)
    return cos_t, sin_t


def _pick(n, prefs):
    for p in prefs:
        if n % p == 0:
            return p
    return n


def kernel(x, c, ctx, c_ctx, w_mod, b_mod, norm_g, ffn_w_in, ffn_w_out, w_in, w_out,
           lru_conv_w, lru_conv_b, lru_w_gates, lru_b_gates, lru_lambda,
           mla_q_norm, mla_w_qb, mla_kv_norm, mla_w_kvb, diff_lambda, diff_norm, gqa_sink):
    batch, seq, d = x.shape
    ctx_len = ctx.shape[1]
    depth = w_mod.shape[0]
    d_ff = ffn_w_out.shape[2]
    n_lat, n_ctx = batch * seq, batch * ctx_len
    n_all = n_lat + n_ctx

    tm_ffn = _pick(math.gcd(seq, n_ctx), (1024, 512, 256))
    tf = _pick(d_ff, (512, 256, 128))
    tm_mix = _pick(math.gcd(seq, n_ctx), (512, 256, 128))
    tq = _pick(math.gcd(seq, ctx_len), (256, 128))
    tq_lat = _pick(seq, (512, 256, 128))
    tq_gqa = _pick(seq, (2 * WINDOW, WINDOW))
    tc = _pick(math.gcd(seq, ctx_len), (256, 128))
    tn_mod = _pick(N_MOD * d, (2048, 1024, 512, 256))
    assert seq % GRID_W == 0 and seq >= tq_gqa + 2 * WINDOW and ctx_len % WINDOW == 0

    def rows_fn(tm):
        per, nlb = seq // tm, n_lat // tm
        return lambda i: jnp.where(i < nlb, i // per, batch)

    def tab_fn(tm):
        per, nlb = seq // tm, n_lat // tm
        return lambda i: jnp.where(i < nlb, i % per, per)

    mod_rows = SUBLANES * ((batch + 1 + SUBLANES - 1) // SUBLANES)
    cvec = jnp.concatenate([c, c_ctx[None, :], jnp.zeros((mod_rows - batch - 1, d), F32)], axis=0)
    mod4 = _modulation(cvec, w_mod, b_mod, tn_mod).reshape(depth, mod_rows, 1, N_MOD * d)

    w_in_t = jnp.swapaxes(w_in, 1, 2).astype(BF16)
    wq_p, wk_p, wmvt_p = _pack_mla(mla_w_qb, mla_w_kvb)
    mla_w = (mla_q_norm[:, None, :], wq_p, mla_kv_norm[:, None, :], wk_p, wmvt_p)
    wg_p = _pack_gates(lru_w_gates)
    norm4 = norm_g[:, :, None, :]
    conv_b3 = lru_conv_b[:, None, :]
    bg4 = lru_b_gates.reshape(depth, 2, 1, 2 * GROUP_W)
    lam4 = lru_lambda[:, :, None, :]
    diff_g3 = diff_norm[:, None, :]
    rows = seq // GRID_W
    tabs = (_rope_tables(rows, HEAD_DIM, 0, HEAD_DIM, tm_mix)
            + _rope_tables(rows, MLA_ROPE, MLA_NOPE, LANES, tm_mix))

    nb_lat, nb_ctx = n_lat // tm_ffn, n_ctx // tm_ffn
    xs = None
    for layer in range(depth):
        last = layer == depth - 1
        lam_init = 0.8 - 0.6 * math.exp(-0.3 * layer)
        ffn = functools.partial(_ffn_sublayer, w_in_all=ffn_w_in, w_out_all=ffn_w_out, layer=layer,
                                row_fn=rows_fn(tm_ffn), tm=tm_ffn, tf=tf, mod4=mod4, norm4=norm4)

        if layer == 0:
            xs = ffn(x.reshape(n_lat, d), nb_lat, n_all, 0, jnp.zeros((n_all, d), F32), k0=0, gi=0, sub=0)
            xs = ffn(ctx.reshape(n_ctx, d), nb_ctx, n_all, nb_lat, xs, k0=0, gi=0, sub=0)
        else:
            xs = ffn(xs, nb_lat + nb_ctx, n_all, 0, None, k0=0, gi=0, sub=0)

        (lru_cols, mq, mk, mvt, dq, dk, dvt, gq, gk, gvt) = _mixer_in(
            xs, mod4, norm4, w_in_t, tabs, mla_w, layer, rows_fn(tm_mix), tab_fn(tm_mix), tm_mix)

        hf, hb = _lru_mixer(lru_cols, lru_conv_w, conv_b3, wg_p, bg4, lam4, layer,
                            batch, seq, ctx_len, tc)

        diff_extra = [diff_lambda, diff_g3]
        diff_specs = [_layer_spec(diff_lambda, layer), _layer_spec(diff_g3, layer)]
        diff_k = functools.partial(_diff_kernel, lam_init=lam_init)
        n_out = n_lat if last else n_all
        fresh = lambda: None if last else jnp.zeros((n_all, GROUP_W), BF16)
        yb = _full_attention(_mla_kernel, mq, mk, mvt, [], [], batch, seq, ctx_len, tq_lat,
                             False, fresh(), n_out)
        yc = _full_attention(diff_k, dq, dk, dvt, diff_specs, diff_extra, batch, seq, ctx_len, tq_lat,
                             False, fresh(), n_out)
        yd = _gqa_attention(gq, gk, gvt, gqa_sink, layer, batch, seq, ctx_len, tq_gqa, False, fresh(), n_out)
        if not last:
            yb = _full_attention(_mla_kernel, mq, mk, mvt, [], [], batch, seq, ctx_len, tq, True, yb, n_out)
            yc = _full_attention(diff_k, dq, dk, dvt, diff_specs, diff_extra, batch, seq, ctx_len, tq,
                                 True, yc, n_out)
            yd = _gqa_attention(gq, gk, gvt, gqa_sink, layer, batch, seq, ctx_len, tq, True, yd, n_out)

        xs = _mixer_out(xs, n_out, mod4, norm4, hf, hb, lru_cols, yb, yc, yd, w_out, layer,
                        rows_fn(tm_mix), tm_mix)
        xs = ffn(xs, n_out // tm_ffn, n_out, 0, None, k0=6, gi=4, sub=1)

    return xs.reshape(batch, seq, d)
```

```python
import functools
import math

import jax
import jax.numpy as jnp
from jax import lax
from jax.experimental import pallas as pl
from jax.experimental.pallas import tpu as pltpu

F32 = jnp.float32
BF16 = jnp.bfloat16

GRID_W = 64
HEAD_DIM = 64
N_MOD = 9
GROUP_W = 512
LRU_HEADS = 8
LRU_BLOCK = 64
LRU_C = 8.0
MLA_HEADS = 8
MLA_NOPE = 64
MLA_ROPE = 32
MLA_V = 64
MLA_Q_RANK = 384
MLA_KV_RANK = 256
DIFF_HEADS = 4
DIFF_D = 64
DIFF_V = 128
GQA_Q_HEADS = 8
GQA_KV_HEADS = 2
GQA_GROUP = 4
WINDOW = 128
MACARON_WEIGHT = 0.5
ROPE_BASE = 10000.0
RMS_EPS = 1e-6
NEG_INF = -1e30
LOG2E = 1.4426950408889634

LANES = 128
SUBLANES = 8
MLA_PAD = 128

COLS_LRU = 2 * GROUP_W
COLS_MLA = MLA_Q_RANK + MLA_KV_RANK + MLA_ROPE
DIFF_QK = 2 * DIFF_HEADS * 2 * DIFF_D
COLS_DIFF = DIFF_QK + DIFF_HEADS * DIFF_V
GQA_QK = (GQA_Q_HEADS + GQA_KV_HEADS) * HEAD_DIM
GQA_KVW = GQA_KV_HEADS * HEAD_DIM
COLS_GQA = GQA_QK + GQA_KVW

VMEM_LIMIT = 60 * 1024 * 1024


def _cparams(sem):
    return pltpu.CompilerParams(dimension_semantics=sem, vmem_limit_bytes=VMEM_LIMIT)


def _rms(x, g):
    ms = jnp.mean(x * x, axis=-1, keepdims=True)
    return x * lax.rsqrt(ms + RMS_EPS) * g


def _dot(a, b):
    return jnp.dot(a, b, preferred_element_type=F32)


def _dot_nt(a, b):
    return lax.dot_general(a, b, (((1,), (1,)), ((), ())), preferred_element_type=F32)


def _layer_spec(arr, layer, *lead):
    shape = arr.shape[1 + len(lead):]
    return pl.BlockSpec((None,) * (1 + len(lead)) + shape,
                        lambda *_: (layer,) + lead + (0,) * len(shape),
                        pipeline_mode=pl.Buffered(1))


def _mod_spec(d, layer, row_fn, k):
    return pl.BlockSpec((None, None, 1, d), lambda i, *_: (layer, row_fn(i), 0, k))


def _skip_first(kernel):
    def wrapped(_, *refs, **kw):
        return kernel(*refs, **kw)
    return wrapped


def _mod_kernel(c_ref, w_ref, b_ref, o_ref):
    s = c_ref[...]
    s = s * jax.nn.sigmoid(s)
    o_ref[...] = _dot(s.astype(BF16), w_ref[...].astype(BF16)) + b_ref[...]


def _modulation(cvec, w_mod, b_mod, tn):
    depth, d, n = w_mod.shape
    rows = cvec.shape[0]
    return pl.pallas_call(
        _mod_kernel,
        grid=(depth, n // tn),
        in_specs=[
            pl.BlockSpec((rows, d), lambda l, j: (0, 0)),
            pl.BlockSpec((None, d, tn), lambda l, j: (l, 0, j)),
            pl.BlockSpec((None, 1, tn), lambda l, j: (l, 0, j)),
        ],
        out_specs=pl.BlockSpec((None, rows, tn), lambda l, j: (l, 0, j)),
        out_shape=jax.ShapeDtypeStruct((depth, rows, n), F32),
        compiler_params=_cparams(("arbitrary", "arbitrary")),
        name="modulation",
    )(cvec, w_mod, b_mod.reshape(depth, 1, n))


FFN_ROW_CHUNK = 64


def _ffn_kernel(x_ref, shift_ref, scale_ref, gate_ref, gpre_ref, gpost_ref,
                wg_ref, wu_ref, wo_ref, o_ref, h_ref, *, nj):
    j = pl.program_id(1)
    n_chunks = x_ref.shape[0] // FFN_ROW_CHUNK

    def rows(r):
        return pl.ds(pl.multiple_of(r * FFN_ROW_CHUNK, FFN_ROW_CHUNK), FFN_ROW_CHUNK)

    @pl.when(j == 0)
    def _():
        mul = gpre_ref[...] * (1.0 + scale_ref[...])
        shift = shift_ref[...]

        def body(r, carry):
            x = x_ref[rows(r), :]
            ms = jnp.mean(x * x, axis=-1, keepdims=True)
            h_ref[rows(r), :] = (x * lax.rsqrt(ms + RMS_EPS) * mul + shift).astype(BF16)
            return carry

        lax.fori_loop(0, n_chunks, body, 0, unroll=2)
        o_ref[...] = jnp.zeros_like(o_ref)

    h = h_ref[...]
    g = _dot(h, wg_ref[...].astype(BF16))
    u = _dot(h, wu_ref[...].astype(BF16))
    a = (g * jax.nn.sigmoid(g) * u).astype(BF16)
    o_ref[...] += _dot(a, wo_ref[...].astype(BF16))

    @pl.when(j == nj - 1)
    def _():
        mul = (MACARON_WEIGHT * gate_ref[...]) * gpost_ref[...]

        def inv_rms(r):
            y = o_ref[rows(r), :]
            return lax.rsqrt(jnp.mean(y * y, axis=-1, keepdims=True) + RMS_EPS)

        def body(r, inv):
            inv_next = inv_rms(jnp.minimum(r + 1, n_chunks - 1))
            o_ref[rows(r), :] = x_ref[rows(r), :] + o_ref[rows(r), :] * inv * mul
            return inv_next

        lax.fori_loop(0, n_chunks, body, inv_rms(0))


def _ffn_sublayer(x_in, n_blocks, out_rows, out_blk0, prev_out, mod4, k0, norm4, gi,
                  w_in_all, w_out_all, layer, sub, row_fn, tm, tf):
    d = x_in.shape[1]
    d_ff = w_out_all.shape[2]
    nj = d_ff // tf
    out_row = lambda i: row_fn(out_blk0 + i)
    g_spec = lambda idx: pl.BlockSpec((None, None, 1, d), lambda i, j: (layer, idx, 0, 0))
    lead_specs, lead_args, aliases, body = [], [], {}, _ffn_kernel
    if prev_out is not None:
        lead_specs, lead_args, aliases = [pl.BlockSpec(memory_space=pl.ANY)], [prev_out], {0: 0}
        body = _skip_first(_ffn_kernel)
    return pl.pallas_call(
        functools.partial(body, nj=nj),
        grid=(n_blocks, nj),
        in_specs=lead_specs + [
            pl.BlockSpec((tm, d), lambda i, j: (i, 0), pipeline_mode=pl.Buffered(1)),
            _mod_spec(d, layer, out_row, k0), _mod_spec(d, layer, out_row, k0 + 1),
            _mod_spec(d, layer, out_row, k0 + 2),
            g_spec(gi), g_spec(gi + 1),
            pl.BlockSpec((None, None, d, tf), lambda i, j: (layer, sub, 0, j)),
            pl.BlockSpec((None, None, d, tf), lambda i, j: (layer, sub, 0, nj + j)),
            pl.BlockSpec((None, None, tf, d), lambda i, j: (layer, sub, j, 0)),
        ],
        out_specs=pl.BlockSpec((tm, d), lambda i, j: (out_blk0 + i, 0), pipeline_mode=pl.Buffered(1)),
        out_shape=jax.ShapeDtypeStruct((out_rows, d), F32),
        scratch_shapes=[pltpu.VMEM((tm, d), BF16)],
        input_output_aliases=aliases,
        compiler_params=_cparams(("arbitrary", "arbitrary")),
        name="ffn_sublayer",
    )(*lead_args, x_in, mod4, mod4, mod4, norm4, norm4, w_in_all, w_in_all, w_out_all)


def _swap_halves(x, first, half):
    return jnp.where(first, pltpu.roll(x, LANES - half, 1), pltpu.roll(x, half, 1))


def _rope_chunks(x, cos, sin, first, half):
    out = []
    for c in range(x.shape[1] // LANES):
        blk = x[:, c * LANES:(c + 1) * LANES]
        out.append(blk * cos + _swap_halves(blk, first, half) * sin)
    return out[0] if len(out) == 1 else jnp.concatenate(out, axis=1)


def _mixer_in_kernel(x_ref, shift_ref, scale_ref, g_ref, wt_ref,
                     cos_hd_ref, sin_hd_ref, cos_mla_ref, sin_mla_ref,
                     qn_ref, wq_ref, kvn_ref, wk_ref, wmvt_ref,
                     lru_ref, mq_ref, mk_ref, mvt_ref, dq_ref, dk_ref, dvt_ref,
                     gq_ref, gk_ref, gvt_ref):
    h = (_rms(x_ref[...], g_ref[...]) * (1.0 + scale_ref[...]) + shift_ref[...]).astype(BF16)
    tm = h.shape[0]
    lane = lax.broadcasted_iota(jnp.int32, (tm, LANES), 1)
    first_hd = (lane % HEAD_DIM) < (HEAD_DIM // 2)
    first_mla = lane < (MLA_NOPE + MLA_ROPE // 2)
    cos_hd, sin_hd = cos_hd_ref[...], sin_hd_ref[...]
    cos_mla, sin_mla = cos_mla_ref[...], sin_mla_ref[...]

    o_mla = COLS_LRU
    o_ckv = o_mla + MLA_Q_RANK
    o_kr = o_ckv + MLA_KV_RANK
    o_diff = o_mla + COLS_MLA
    o_gqa = o_diff + COLS_DIFF
    proj = lambda a, b: _dot_nt(h, wt_ref[a:b, :])

    lru_ref[...] = proj(0, COLS_LRU)

    qn = _rms(proj(o_mla, o_ckv), qn_ref[...]).astype(BF16)
    q = _rope_chunks(_dot(qn, wq_ref[...]), cos_mla, sin_mla, first_mla, MLA_ROPE // 2)
    mq_ref[...] = (q * (LOG2E * (MLA_NOPE + MLA_ROPE) ** -0.5)).astype(BF16)
    kvn = _rms(proj(o_ckv, o_kr), kvn_ref[...]).astype(BF16)
    kr = pltpu.roll(proj(o_kr, o_kr + LANES), MLA_NOPE, 1)
    kr = jnp.where((lane >= MLA_NOPE) & (lane < MLA_NOPE + MLA_ROPE), kr, 0.0)
    k_rope = _rope_chunks(kr, cos_mla, sin_mla, first_mla, MLA_ROPE // 2)
    k_nope = _dot(kvn, wk_ref[...])
    mk_ref[...] = (k_nope + jnp.concatenate([k_rope] * MLA_HEADS, axis=1)).astype(BF16)
    mvt_ref[...] = _dot_nt(wmvt_ref[...], kvn).astype(BF16)

    cd = proj(o_diff, o_diff + DIFF_QK)
    half = DIFF_QK // 2
    dq_ref[...] = (_rope_chunks(cd[:, :half], cos_hd, sin_hd, first_hd, HEAD_DIM // 2)
                   * (LOG2E * DIFF_D ** -0.5)).astype(BF16)
    dk_ref[...] = _rope_chunks(cd[:, half:], cos_hd, sin_hd, first_hd, HEAD_DIM // 2).astype(BF16)

    nq = GQA_Q_HEADS * HEAD_DIM
    cg = proj(o_gqa, o_gqa + GQA_QK)
    gq_ref[...] = (_rope_chunks(cg[:, :nq], cos_hd, sin_hd, first_hd, HEAD_DIM // 2)
                   * (LOG2E * HEAD_DIM ** -0.5)).astype(BF16)
    gk_ref[...] = _rope_chunks(cg[:, nq:], cos_hd, sin_hd, first_hd, HEAD_DIM // 2).astype(BF16)

    dvt_ref[...] = _dot_nt(wt_ref[o_diff + DIFF_QK:o_gqa, :], h).astype(BF16)
    gvt_ref[...] = _dot_nt(wt_ref[o_gqa + GQA_QK:o_gqa + COLS_GQA, :], h).astype(BF16)


def _mixer_in(xs, mod4, norm4, w_in_t, tabs, mla_w, layer, row_fn, tab_fn, tm):
    n_rows, d = xs.shape
    cos_hd, sin_hd, cos_mla, sin_mla = tabs
    qn, wq, kvn, wk, wmvt = mla_w
    tab_spec = pl.BlockSpec((tm, LANES), lambda i: (tab_fn(i), 0))
    outs = [(COLS_LRU, F32, False),
            (MLA_HEADS * MLA_PAD, BF16, False), (MLA_HEADS * MLA_PAD, BF16, False),
            (GROUP_W, BF16, True),
            (GROUP_W, BF16, False), (GROUP_W, BF16, False), (GROUP_W, BF16, True),
            (GROUP_W, BF16, False), (GQA_KVW, BF16, False), (GQA_KVW, BF16, True)]
    out_specs = [pl.BlockSpec((w, tm), lambda i: (0, i)) if t else pl.BlockSpec((tm, w), lambda i: (i, 0))
                 for w, _, t in outs]
    out_shape = [jax.ShapeDtypeStruct((w, n_rows) if t else (n_rows, w), dt) for w, dt, t in outs]
    return pl.pallas_call(
        _mixer_in_kernel,
        grid=(n_rows // tm,),
        in_specs=[
            pl.BlockSpec((tm, d), lambda i: (i, 0)),
            _mod_spec(d, layer, row_fn, 3), _mod_spec(d, layer, row_fn, 4),
            _layer_spec(norm4, layer, 2), _layer_spec(w_in_t, layer),
            tab_spec, tab_spec, tab_spec, tab_spec,
            _layer_spec(qn, layer), _layer_spec(wq, layer), _layer_spec(kvn, layer),
            _layer_spec(wk, layer), _layer_spec(wmvt, layer),
        ],
        out_specs=out_specs,
        out_shape=out_shape,
        compiler_params=_cparams(("arbitrary",)),
        name="mixer_in",
    )(xs, mod4, mod4, norm4, w_in_t, cos_hd, sin_hd, cos_mla, sin_mla, qn, wq, kvn, wk, wmvt)


def _sigmoid(x):
    return 0.5 * jnp.tanh(0.5 * x) + 0.5


def _lru_direction(x, prev, nxt, cw, cb, wg, bg, lam, reverse, carry_ref, h_ref, a_scr, b_scr):
    tc, ch = x.shape
    ext = jnp.concatenate([prev, x, nxt], axis=0)
    n_ext = tc + 2 * SUBLANES

    def tap(shift):
        return pltpu.roll(ext, (-shift) % n_ext, 0)[SUBLANES:SUBLANES + tc]

    u = cw[0:1] * tap(-2) + cw[1:2] * tap(-1) + cw[2:3] * x + cw[3:4] * tap(1) + cb
    g = _dot(u.astype(BF16), wg) + bg
    r = _sigmoid(g[:, :ch])
    i = _sigmoid(g[:, ch:])
    log_a = -LRU_C * r * jax.nn.softplus(-lam)
    a = jnp.exp(log_a)
    b = jnp.sqrt((1.0 - a) * (1.0 + a)) * (i * u)

    nt = tc // SUBLANES
    a = a.reshape(nt, SUBLANES, ch)
    b = b.reshape(nt, SUBLANES, ch)
    row = lax.broadcasted_iota(jnp.int32, (nt, SUBLANES, ch), 1)
    for s in (1, 2, 4):
        ok = (row < SUBLANES - s) if reverse else (row >= s)
        shift = SUBLANES - s if reverse else s
        a_sh, b_sh = pltpu.roll(a, shift, 1), pltpu.roll(b, shift, 1)
        b = a * jnp.where(ok, b_sh, 0.0) + b
        a = a * jnp.where(ok, a_sh, 1.0)
    a_scr[...] = a.reshape(tc, ch)
    b_scr[...] = b.reshape(tc, ch)

    def body(k, carry):
        kk = nt - 1 - k if reverse else k
        sl = pl.ds(pl.multiple_of(kk * SUBLANES, SUBLANES), SUBLANES)
        ht = b_scr[sl, :] + a_scr[sl, :] * carry
        h_ref[sl, :] = ht
        edge = ht[0:1, :] if reverse else ht[SUBLANES - 1:SUBLANES, :]
        return jnp.broadcast_to(edge, (SUBLANES, ch))

    carry_ref[...] = lax.fori_loop(0, nt, body, carry_ref[...])


def _lru_kernel(xf_ref, pf_ref, nf_ref, xb_ref, pb_ref, nb_ref,
                cw_ref, cb_ref, wgf_ref, wgb_ref, bgf_ref, bgb_ref, lamf_ref, lamb_ref,
                hf_ref, hb_ref, cf_ref, cbk_ref, a_scr, b_scr, *, halo_fn):
    s = pl.program_id(1)

    @pl.when(s == 0)
    def _():
        cf_ref[...] = jnp.zeros_like(cf_ref)
        cbk_ref[...] = jnp.zeros_like(cbk_ref)

    pvf, nvf, pvb, nvb = halo_fn(s)
    cw, cb = cw_ref[...], cb_ref[...]
    _lru_direction(xf_ref[...], pf_ref[...] * pvf, nf_ref[...] * nvf, cw, cb,
                   wgf_ref[...], bgf_ref[...], lamf_ref[...], False, cf_ref, hf_ref, a_scr, b_scr)
    _lru_direction(xb_ref[...], pb_ref[...] * pvb, nb_ref[...] * nvb, cw, cb,
                   wgb_ref[...], bgb_ref[...], lamb_ref[...], True, cbk_ref, hb_ref, a_scr, b_scr)


def _lru_mixer(lru_cols, conv_w, conv_b3, wg, bg4, lam4, layer, batch, seq, ctx_len, tc):
    n_rows = lru_cols.shape[0]
    ch = GROUP_W
    nl, ncx = seq // tc, ctx_len // tc
    n_steps = nl + ncx
    r8 = tc // SUBLANES
    last8 = n_rows // SUBLANES - 1

    def chunk_f(b, s):
        return jnp.where(s < ncx, batch * nl + b * ncx + s, b * nl + s - ncx)

    def chunk_b(b, s):
        return jnp.where(s < ncx, batch * nl + b * ncx + ncx - 1 - s, b * nl + nl - 1 - (s - ncx))

    def seg_pos(s, reverse):
        in_ctx = s < ncx
        if reverse:
            idx = jnp.where(in_ctx, ncx - 1 - s, nl - 1 - (s - ncx))
        else:
            idx = jnp.where(in_ctx, s, s - ncx)
        return idx, jnp.where(in_ctx, ncx, nl)

    def halo_fn(s):
        out = []
        for reverse in (False, True):
            idx, n = seg_pos(s, reverse)
            out += [(idx > 0).astype(F32), (idx < n - 1).astype(F32)]
        return out

    def x_spec(fn):
        return pl.BlockSpec((tc, ch), lambda b, s: (fn(b, s), 0))

    def prev_spec(fn):
        return pl.BlockSpec((SUBLANES, ch), lambda b, s: (jnp.maximum(fn(b, s) * r8 - 1, 0), 0))

    def next_spec(fn):
        return pl.BlockSpec((SUBLANES, ch), lambda b, s: (jnp.minimum((fn(b, s) + 1) * r8, last8), 0))

    return pl.pallas_call(
        functools.partial(_lru_kernel, halo_fn=halo_fn),
        grid=(batch, n_steps),
        in_specs=[
            x_spec(chunk_f), prev_spec(chunk_f), next_spec(chunk_f),
            x_spec(chunk_b), prev_spec(chunk_b), next_spec(chunk_b),
            _layer_spec(conv_w, layer), _layer_spec(conv_b3, layer),
            _layer_spec(wg, layer, 0), _layer_spec(wg, layer, 1),
            _layer_spec(bg4, layer, 0), _layer_spec(bg4, layer, 1),
            _layer_spec(lam4, layer, 0), _layer_spec(lam4, layer, 1),
        ],
        out_specs=[x_spec(chunk_f), x_spec(chunk_b)],
        out_shape=[jax.ShapeDtypeStruct((n_rows, ch), F32)] * 2,
        scratch_shapes=[pltpu.VMEM((SUBLANES, ch), F32), pltpu.VMEM((SUBLANES, ch), F32),
                        pltpu.VMEM((tc, ch), F32), pltpu.VMEM((tc, ch), F32)],
        compiler_params=_cparams(("arbitrary", "arbitrary")),
        name="rglru",
    )(lru_cols, lru_cols, lru_cols, lru_cols, lru_cols, lru_cols,
      conv_w, conv_b3, wg, wg, bg4, bg4, lam4, lam4)


def _softmax_cols(scores, extra=None):
    m = scores[0].max(axis=0, keepdims=True)
    for s in scores[1:]:
        m = jnp.maximum(m, s.max(axis=0, keepdims=True))
    if extra is not None:
        m = jnp.maximum(m, extra)
    ps = [jnp.exp2(s - m) for s in scores]
    den = ps[0].sum(axis=0, keepdims=True)
    for p in ps[1:]:
        den = den + p.sum(axis=0, keepdims=True)
    if extra is not None:
        den = den + jnp.exp2(extra - m)
    return ps, den


def _two_block_queries(q2, width):
    lane = lax.broadcasted_iota(jnp.int32, q2.shape, 1)
    zero = jnp.zeros((), q2.dtype)
    return jnp.concatenate([jnp.where(lane < width, q2, zero), jnp.where(lane >= width, q2, zero)], axis=0)


def _paired_probs(q_ref, ks, width, n_iter, finish, s_bufs):
    def issue(i):
        sl = slice(2 * i * width, 2 * (i + 1) * width)
        qq = _two_block_queries(q_ref[:, sl], width)
        buf, row, m = s_bufs[i % 2], 0, None
        for k in ks:
            s = _dot_nt(k[:, sl], qq)
            buf[row:row + k.shape[0], :] = s
            row += k.shape[0]
            ms = s.max(axis=0, keepdims=True)
            m = ms if m is None else jnp.maximum(m, ms)
        return m

    m_next = issue(0)
    for i in range(n_iter):
        m = m_next
        if i + 1 < n_iter:
            m_next = issue(i + 1)
        p = jnp.exp2(s_bufs[i % 2][...] - m)
        finish(i, p, p.sum(axis=0, keepdims=True))


def _weighted_values(vts, rows, p):
    o, row = None, 0
    for vt in vts:
        t = _dot(vt[rows, :], p[row:row + vt.shape[1], :])
        o = t if o is None else o + t
        row += vt.shape[1]
    return o


def _mla_kernel(q_ref, *refs, n_seg):
    ks, vts = refs[:n_seg], refs[n_seg:2 * n_seg]
    o_ref, s_bufs = refs[2 * n_seg], refs[2 * n_seg + 1:]
    tq = q_ref.shape[0]
    outs = []

    def finish(hp, p, den):
        p = p.astype(BF16)
        for e in range(2):
            h = 2 * hp + e
            cols = slice(e * tq, (e + 1) * tq)
            o = _weighted_values(vts, slice(h * MLA_V, (h + 1) * MLA_V), p[:, cols])
            outs.append(o / den[:, cols])

    _paired_probs(q_ref, ks, MLA_PAD, MLA_HEADS // 2, finish, s_bufs)
    o_ref[...] = jnp.concatenate(outs, axis=0).T.astype(o_ref.dtype)


def _diff_kernel(q_ref, lv_ref, g_ref, *refs, n_seg, lam_init):
    ks, vts = refs[:n_seg], refs[n_seg:2 * n_seg]
    o_ref, s_bufs = refs[2 * n_seg], refs[2 * n_seg + 1:]
    lv = lv_ref[...]
    lam = (jnp.exp(jnp.sum(lv[0:1] * lv[1:2], axis=-1, keepdims=True))
           - jnp.exp(jnp.sum(lv[2:3] * lv[3:4], axis=-1, keepdims=True)) + lam_init)
    tq = q_ref.shape[0]
    outs = []

    def finish(h, p, den):
        c1 = 1.0 / den[:, :tq]
        c2 = lam / den[:, tq:]
        w = (p[:, :tq] * c1 - p[:, tq:] * c2).astype(BF16)
        outs.append(_weighted_values(vts, slice(h * DIFF_V, (h + 1) * DIFF_V), w))

    _paired_probs(q_ref, ks, DIFF_D, DIFF_HEADS, finish, s_bufs)
    o = jnp.concatenate(outs, axis=0).T
    g = g_ref[...]
    for h in range(DIFF_HEADS):
        sl = slice(h * DIFF_V, (h + 1) * DIFF_V)
        o_ref[:, sl] = (_rms(o[:, sl], g) * (1.0 - lam_init)).astype(o_ref.dtype)


def _into(buf):
    if buf is None:
        return [], [], {}
    return [pl.BlockSpec(memory_space=pl.ANY)], [buf], {0: 0}


def _full_attention(kernel, q, k, vt, extra_specs, extra, batch, seq, ctx_len, tq,
                    ctx_queries, into, out_rows):
    qw, kw, vw = q.shape[1], k.shape[1], vt.shape[0]
    ctx_blk0 = batch * seq // ctx_len

    ctx_k = pl.BlockSpec((ctx_len, kw), lambda b, i: (ctx_blk0 + b, 0))
    ctx_v = pl.BlockSpec((vw, ctx_len), lambda b, i: (0, ctx_blk0 + b))
    lat_k = pl.BlockSpec((seq, kw), lambda b, i: (b, 0))
    lat_v = pl.BlockSpec((vw, seq), lambda b, i: (0, b))
    if ctx_queries:
        nq = ctx_len // tq
        q_map = lambda b, i: (batch * seq // tq + b * nq + i, 0)
        kv_specs, kv_args, n_seg = [ctx_k, ctx_v], [k, vt], 1
    else:
        nq = seq // tq
        q_map = lambda b, i: (b * nq + i, 0)
        kv_specs, kv_args, n_seg = [lat_k, ctx_k, lat_v, ctx_v], [k, k, vt, vt], 2
    lead_specs, lead_args, aliases = _into(into)
    body = functools.partial(kernel if into is None else _skip_first(kernel), n_seg=n_seg)
    n_keys = ctx_len if ctx_queries else seq + ctx_len
    return pl.pallas_call(
        body,
        grid=(batch, nq),
        in_specs=lead_specs + [pl.BlockSpec((tq, qw), q_map)] + extra_specs + kv_specs,
        out_specs=pl.BlockSpec((tq, GROUP_W), q_map),
        out_shape=jax.ShapeDtypeStruct((out_rows, GROUP_W), BF16),
        scratch_shapes=[pltpu.VMEM((n_keys, 2 * tq), F32)] * 2,
        input_output_aliases=aliases,
        compiler_params=_cparams(("arbitrary", "arbitrary")),
        name="full_attention",
    )(*lead_args, q, *extra, *kv_args)


def _gqa_group(q_ref, kv, ks, vts, masks, sinks, outs):
    ksl = slice(kv * HEAD_DIM, (kv + 1) * HEAD_DIM)
    heads = range(kv * GQA_GROUP, (kv + 1) * GQA_GROUP)
    nq = q_ref.shape[0]
    q4 = jnp.concatenate([q_ref[:, h * HEAD_DIM:(h + 1) * HEAD_DIM] for h in heads], axis=0)
    scores = []
    for k, mask in zip(ks, masks):
        s = _dot_nt(k[:, ksl], q4)
        scores.append(s if mask is None else jnp.where(mask, s, NEG_INF))
    sink = jnp.concatenate([jnp.full((1, nq), sinks(h) * LOG2E, F32) for h in heads], axis=1)
    ps, den = _softmax_cols(scores, extra=sink)
    o = None
    for p, vt in zip(ps, vts):
        t = _dot(vt[ksl, :], p.astype(BF16))
        o = t if o is None else o + t
    o = o / den
    outs += [o[:, g * nq:(g + 1) * nq] for g in range(GQA_GROUP)]


def _gqa_lat_kernel(sink_ref, q_ref, kl_ref, vtl_ref, kc_ref, vtc_ref, o_ref, *, seq, layer):
    n = pl.program_id(1)
    tq = q_ref.shape[0]
    span = tq + 2 * WINDOW
    start = pl.multiple_of(jnp.clip(n * tq - WINDOW, 0, seq - span), WINDOW)
    kw = kl_ref[pl.ds(start, span), :]
    vtw = vtl_ref[:, pl.ds(start, span)]
    kpos = start + lax.broadcasted_iota(jnp.int32, (span, GQA_GROUP * tq), 0)
    qpos = n * tq + lax.broadcasted_iota(jnp.int32, (span, GQA_GROUP * tq), 1) % tq
    valid = jnp.abs(kpos - qpos) <= WINDOW
    sinks = lambda h: sink_ref[layer, h]
    outs = []
    for kv in range(GQA_KV_HEADS):
        _gqa_group(q_ref, kv, [kc_ref[...], kw], [vtc_ref[...], vtw], [None, valid], sinks, outs)
    o_ref[...] = jnp.concatenate(outs, axis=0).T.astype(o_ref.dtype)


def _gqa_ctx_kernel(sink_ref, q_ref, kc_ref, vtc_ref, o_ref, *, layer):
    sinks = lambda h: sink_ref[layer, h]
    outs = []
    for kv in range(GQA_KV_HEADS):
        _gqa_group(q_ref, kv, [kc_ref[...]], [vtc_ref[...]], [None], sinks, outs)
    o_ref[...] = jnp.concatenate(outs, axis=0).T.astype(o_ref.dtype)


def _gqa_attention(q, k, vt, sink, layer, batch, seq, ctx_len, tq, ctx_queries, into, out_rows):
    ctx_blk0 = batch * seq // ctx_len
    smem = pl.BlockSpec(memory_space=pltpu.SMEM)
    ctx_k = pl.BlockSpec((ctx_len, GQA_KVW), lambda b, i: (ctx_blk0 + b, 0))
    ctx_v = pl.BlockSpec((GQA_KVW, ctx_len), lambda b, i: (0, ctx_blk0 + b))
    if ctx_queries:
        nq = ctx_len // tq
        q_map = lambda b, i: (batch * seq // tq + b * nq + i, 0)
        kernel = functools.partial(_gqa_ctx_kernel, layer=layer)
        specs = [smem, pl.BlockSpec((tq, GROUP_W), q_map), ctx_k, ctx_v]
        args = [sink, q, k, vt]
    else:
        nq = seq // tq
        q_map = lambda b, i: (b * nq + i, 0)
        lat_k = pl.BlockSpec((seq, GQA_KVW), lambda b, i: (b, 0))
        lat_v = pl.BlockSpec((GQA_KVW, seq), lambda b, i: (0, b))
        kernel = functools.partial(_gqa_lat_kernel, seq=seq, layer=layer)
        specs = [smem, pl.BlockSpec((tq, GROUP_W), q_map), lat_k, lat_v, ctx_k, ctx_v]
        args = [sink, q, k, vt, k, vt]
    lead_specs, lead_args, aliases = _into(into)
    if into is not None:
        kernel = _skip_first(kernel)
    return pl.pallas_call(
        kernel,
        grid=(batch, nq),
        in_specs=lead_specs + specs,
        out_specs=pl.BlockSpec((tq, GROUP_W), q_map),
        out_shape=jax.ShapeDtypeStruct((out_rows, GROUP_W), BF16),
        input_output_aliases=aliases,
        compiler_params=_cparams(("arbitrary", "arbitrary")),
        name="gqa_attention",
    )(*lead_args, *args)


def _gqa_both_kernel(sink_ref, q_ref, kl_ref, vtl_ref, kc_ref, vtc_ref, o_ref, *, seq, layer, nq_lat):
    i = pl.program_id(1)

    @pl.when(i < nq_lat)
    def _():
        _gqa_lat_kernel(sink_ref, q_ref, kl_ref, vtl_ref, kc_ref, vtc_ref, o_ref, seq=seq, layer=layer)

    @pl.when(i >= nq_lat)
    def _():
        _gqa_ctx_kernel(sink_ref, q_ref, kc_ref, vtc_ref, o_ref, layer=layer)


def _gqa_both(q, k, vt, sink, layer, batch, seq, ctx_len, tq):
    n_rows = q.shape[0]
    ctx_blk0 = batch * seq // ctx_len
    nq_lat, nq_ctx = seq // tq, ctx_len // tq
    q_map = lambda b, i: (jnp.where(i < nq_lat, b * nq_lat + i,
                                    batch * nq_lat + b * nq_ctx + i - nq_lat), 0)
    return pl.pallas_call(
        functools.partial(_gqa_both_kernel, seq=seq, layer=layer, nq_lat=nq_lat),
        grid=(batch, nq_lat + nq_ctx),
        in_specs=[pl.BlockSpec(memory_space=pltpu.SMEM), pl.BlockSpec((tq, GROUP_W), q_map),
                  pl.BlockSpec((seq, GQA_KVW), lambda b, i: (b, 0)),
                  pl.BlockSpec((GQA_KVW, seq), lambda b, i: (0, b)),
                  pl.BlockSpec((ctx_len, GQA_KVW), lambda b, i: (ctx_blk0 + b, 0)),
                  pl.BlockSpec((GQA_KVW, ctx_len), lambda b, i: (0, ctx_blk0 + b))],
        out_specs=pl.BlockSpec((tq, GROUP_W), q_map),
        out_shape=jax.ShapeDtypeStruct((n_rows, GROUP_W), BF16),
        compiler_params=_cparams(("arbitrary", "arbitrary")),
        name="gqa_attention",
    )(sink, q, k, vt, k, vt)


def _mixer_out_kernel(x_ref, gate_ref, g_ref, hf_ref, hb_ref, lg_ref, yb_ref, yc_ref, yd_ref,
                      w_ref, o_ref):
    ya = ((hf_ref[...] + hb_ref[...]) * jax.nn.gelu(lg_ref[...])).astype(BF16)
    w = lambda g: w_ref[g * GROUP_W:(g + 1) * GROUP_W, :].astype(BF16)
    y = _dot(ya, w(0))
    y += _dot(yb_ref[...], w(1))
    y += _dot(yc_ref[...], w(2))
    y += _dot(yd_ref[...], w(3))
    o_ref[...] = x_ref[...] + gate_ref[...] * _rms(y, g_ref[...])


def _mixer_out(xs, n_rows, mod4, norm4, hf, hb, lru_cols, yb, yc, yd, w_out, layer, row_fn, tm):
    d = xs.shape[1]
    row_spec = pl.BlockSpec((tm, GROUP_W), lambda i: (i, 0))
    return pl.pallas_call(
        _mixer_out_kernel,
        grid=(n_rows // tm,),
        in_specs=[
            pl.BlockSpec((tm, d), lambda i: (i, 0)),
            _mod_spec(d, layer, row_fn, 5), _layer_spec(norm4, layer, 3),
            row_spec, row_spec,
            pl.BlockSpec((tm, GROUP_W), lambda i: (i, 1)),
            row_spec, row_spec, row_spec,
            _layer_spec(w_out, layer),
        ],
        out_specs=pl.BlockSpec((tm, d), lambda i: (i, 0)),
        out_shape=jax.ShapeDtypeStruct((n_rows, d), F32),
        compiler_params=_cparams(("arbitrary",)),
        name="mixer_out",
    )(xs, mod4, norm4, hf, hb, lru_cols, yb, yc, yd, w_out)


def _pack_mla(w_qb, w_kvb):
    depth = w_qb.shape[0]
    hd = MLA_NOPE + MLA_ROPE
    wq = w_qb.reshape(depth, MLA_Q_RANK, MLA_HEADS, hd)
    wq = jnp.pad(wq, ((0, 0), (0, 0), (0, 0), (0, MLA_PAD - hd)))
    wkv = w_kvb.reshape(depth, MLA_KV_RANK, MLA_HEADS, MLA_NOPE + MLA_V)
    wk = jnp.pad(wkv[..., :MLA_NOPE], ((0, 0), (0, 0), (0, 0), (0, MLA_PAD - MLA_NOPE)))
    wv = wkv[..., MLA_NOPE:].reshape(depth, MLA_KV_RANK, MLA_HEADS * MLA_V)
    return (wq.reshape(depth, MLA_Q_RANK, MLA_HEADS * MLA_PAD).astype(BF16),
            wk.reshape(depth, MLA_KV_RANK, MLA_HEADS * MLA_PAD).astype(BF16),
            jnp.swapaxes(wv, 1, 2).astype(BF16))


def _pack_gates(w_gates):
    depth = w_gates.shape[0]
    eye = jnp.eye(LRU_HEADS, dtype=w_gates.dtype)
    dense = jnp.einsum('ldghij,hk->ldghikj', w_gates, eye).reshape(depth, 2, 2, GROUP_W, GROUP_W)
    return jnp.concatenate([dense[:, :, 0], dense[:, :, 1]], axis=-1).astype(BF16)


def _rope_tables(rows, rot_dim, lane0, period, pad_rows):
    row = jnp.repeat(jnp.arange(rows, dtype=F32), GRID_W)
    col = jnp.tile(jnp.arange(GRID_W, dtype=F32), rows)
    per_axis = rot_dim // 2
    inv = ROPE_BASE ** (-jnp.arange(0, per_axis, 2, dtype=F32) / per_axis)
    ang = jnp.concatenate([row[:, None] * inv, col[:, None] * inv], axis=-1)
    cos, sin = jnp.cos(ang), jnp.sin(ang)
    lane = jnp.arange(LANES)
    rel = (lane - lane0) % period
    active = (lane >= lane0) & (rel < rot_dim)
    idx = rel % per_axis
    sign = jnp.where(rel < per_axis, -1.0, 1.0)
    cos_t = jnp.where(active[None, :], cos[:, idx], 1.0)
    sin_t = jnp.where(active[None, :], sin[:, idx] * sign[None, :], 0.0)
    cos_t = jnp.concatenate([cos_t, jnp.ones((pad_rows, LANES), F32)], axis=0)
    sin_t = jnp.concatenate([sin_t, jnp.zeros((pad_rows, LANES), F32)], axis=0)
    return cos_t, sin_t


def _pick(n, prefs):
    for p in prefs:
        if n % p == 0:
            return p
    return n


def kernel(x, c, ctx, c_ctx, w_mod, b_mod, norm_g, ffn_w_in, ffn_w_out, w_in, w_out,
           lru_conv_w, lru_conv_b, lru_w_gates, lru_b_gates, lru_lambda,
           mla_q_norm, mla_w_qb, mla_kv_norm, mla_w_kvb, diff_lambda, diff_norm, gqa_sink):
    batch, seq, d = x.shape
    ctx_len = ctx.shape[1]
    depth = w_mod.shape[0]
    d_ff = ffn_w_out.shape[2]
    n_lat, n_ctx = batch * seq, batch * ctx_len
    n_all = n_lat + n_ctx

    tm_ffn = _pick(math.gcd(seq, n_ctx), (1024, 512, 256))
    tf = _pick(d_ff, (512, 256, 128))
    tm_mix = _pick(math.gcd(seq, n_ctx), (512, 256, 128))
    tq = _pick(math.gcd(seq, ctx_len), (256, 128))
    tq_lat = _pick(seq, (512, 256, 128))
    tq_gqa = _pick(seq, (2 * WINDOW, WINDOW))
    tc = _pick(math.gcd(seq, ctx_len), (256, 128))
    tn_mod = _pick(N_MOD * d, (2048, 1024, 512, 256))
    assert seq % GRID_W == 0 and seq >= tq_gqa + 2 * WINDOW and ctx_len % WINDOW == 0

    def rows_fn(tm):
        per, nlb = seq // tm, n_lat // tm
        return lambda i: jnp.where(i < nlb, i // per, batch)

    def tab_fn(tm):
        per, nlb = seq // tm, n_lat // tm
        return lambda i: jnp.where(i < nlb, i % per, per)

    mod_rows = SUBLANES * ((batch + 1 + SUBLANES - 1) // SUBLANES)
    cvec = jnp.concatenate([c, c_ctx[None, :], jnp.zeros((mod_rows - batch - 1, d), F32)], axis=0)
    mod4 = _modulation(cvec, w_mod, b_mod, tn_mod).reshape(depth, mod_rows, 1, N_MOD * d)

    w_in_t = jnp.swapaxes(w_in, 1, 2).astype(BF16)
    wq_p, wk_p, wmvt_p = _pack_mla(mla_w_qb, mla_w_kvb)
    mla_w = (mla_q_norm[:, None, :], wq_p, mla_kv_norm[:, None, :], wk_p, wmvt_p)
    wg_p = _pack_gates(lru_w_gates)
    norm4 = norm_g[:, :, None, :]
    conv_b3 = lru_conv_b[:, None, :]
    bg4 = lru_b_gates.reshape(depth, 2, 1, 2 * GROUP_W)
    lam4 = lru_lambda[:, :, None, :]
    diff_g3 = diff_norm[:, None, :]
    rows = seq // GRID_W
    tabs = (_rope_tables(rows, HEAD_DIM, 0, HEAD_DIM, tm_mix)
            + _rope_tables(rows, MLA_ROPE, MLA_NOPE, LANES, tm_mix))

    nb_lat, nb_ctx = n_lat // tm_ffn, n_ctx // tm_ffn
    xs = None
    for layer in range(depth):
        last = layer == depth - 1
        lam_init = 0.8 - 0.6 * math.exp(-0.3 * layer)
        ffn = functools.partial(_ffn_sublayer, w_in_all=ffn_w_in, w_out_all=ffn_w_out, layer=layer,
                                row_fn=rows_fn(tm_ffn), tm=tm_ffn, tf=tf, mod4=mod4, norm4=norm4)

        if layer == 0:
            xs = ffn(x.reshape(n_lat, d), nb_lat, n_all, 0, jnp.zeros((n_all, d), F32), k0=0, gi=0, sub=0)
            xs = ffn(ctx.reshape(n_ctx, d), nb_ctx, n_all, nb_lat, xs, k0=0, gi=0, sub=0)
        else:
            xs = ffn(xs, nb_lat + nb_ctx, n_all, 0, None, k0=0, gi=0, sub=0)

        (lru_cols, mq, mk, mvt, dq, dk, dvt, gq, gk, gvt) = _mixer_in(
            xs, mod4, norm4, w_in_t, tabs, mla_w, layer, rows_fn(tm_mix), tab_fn(tm_mix), tm_mix)

        hf, hb = _lru_mixer(lru_cols, lru_conv_w, conv_b3, wg_p, bg4, lam4, layer,
                            batch, seq, ctx_len, tc)

        diff_extra = [diff_lambda, diff_g3]
        diff_specs = [_layer_spec(diff_lambda, layer), _layer_spec(diff_g3, layer)]
        diff_k = functools.partial(_diff_kernel, lam_init=lam_init)
        n_out = n_lat if last else n_all
        fresh = lambda: None if last else jnp.zeros((n_all, GROUP_W), BF16)
        yb = _full_attention(_mla_kernel, mq, mk, mvt, [], [], batch, seq, ctx_len, tq_lat,
                             False, fresh(), n_out)
        yc = _full_attention(diff_k, dq, dk, dvt, diff_specs, diff_extra, batch, seq, ctx_len, tq_lat,
                             False, fresh(), n_out)
        if last:
            yd = _gqa_attention(gq, gk, gvt, gqa_sink, layer, batch, seq, ctx_len, tq_gqa, False, None, n_out)
        else:
            yd = _gqa_both(gq, gk, gvt, gqa_sink, layer, batch, seq, ctx_len, tq_gqa)
        if not last:
            yb = _full_attention(_mla_kernel, mq, mk, mvt, [], [], batch, seq, ctx_len, tq, True, yb, n_out)
            yc = _full_attention(diff_k, dq, dk, dvt, diff_specs, diff_extra, batch, seq, ctx_len, tq,
                                 True, yc, n_out)

        xs = _mixer_out(xs, n_out, mod4, norm4, hf, hb, lru_cols, yb, yc, yd, w_out, layer,
                        rows_fn(tm_mix), tm_mix)
        xs = ffn(xs, n_out // tm_ffn, n_out, 0, None, k0=6, gi=4, sub=1)

    return xs.reshape(batch, seq, d)
```
